```python
import math
import jax, jax.numpy as jnp
from jax import lax
import numpy as np

D_MODEL = 2048
BATCH = 2
SEQ = 8192
DEPTH = 2

N_MIXERS = 2
ATT_GROUPS = ((128, 1), (512, 4), (2048, 16))
N_GROUPS = len(ATT_GROUPS)
HEAD_DIM = 128
HEADS_PER_GROUP = D_MODEL // HEAD_DIM
ATT_WIDTH = HEADS_PER_GROUP * HEAD_DIM
A_IN_COLS = 3 * N_GROUPS * ATT_WIDTH + ATT_WIDTH
BLOCK = 128
N_BUCKETS = 32
MAX_DISTANCE = 2048
CONV_WIDTH = 31
CONV_CH = D_MODEL
B_IN_COLS = 3 * CONV_CH
EPS = 1e-6
N_A = (DEPTH + 1) // 2
N_B = DEPTH // 2

kernel_name = "hybrid_dilated_attn_conformer_conv"


def _rmsnorm(x, g):
    xf = x.astype(jnp.float32)
    y = xf * lax.rsqrt(jnp.mean(xf * xf, axis=-1, keepdims=True) + EPS) * g.astype(jnp.float32)
    return y.astype(x.dtype)


def _head_rms(t, g):
    tf = t.astype(jnp.float32)
    return tf * lax.rsqrt(jnp.mean(tf * tf, axis=-1, keepdims=True) + EPS) * g.astype(jnp.float32)


def _t5_bucket(dist):
    max_exact = N_BUCKETS // 2
    is_small = dist < max_exact
    ratio = jnp.log(jnp.maximum(dist, 1).astype(jnp.float32) / max_exact) / math.log(MAX_DISTANCE / max_exact)
    large = max_exact + (ratio * (N_BUCKETS - max_exact)).astype(jnp.int32)
    large = jnp.minimum(large, N_BUCKETS - 1)
    return jnp.where(is_small, dist, large)


def _dilated_window_attention(q, k, v, bias_table, window, dilation):
    B, S, H, Dh = q.shape
    steps = window // dilation
    L = S // dilation
    Lp = -(-L // BLOCK) * BLOCK
    nb = Lp // BLOCK

    def gather(t):
        t = t.reshape(B, L, dilation, H, Dh).transpose(0, 2, 1, 3, 4).reshape(B * dilation, L, H, Dh)
        t = jnp.pad(t, ((0, 0), (0, Lp - L), (0, 0), (0, 0)))
        return t.reshape(B * dilation, nb, BLOCK, H, Dh)

    def with_prev(t):
        prev = jnp.pad(t[:, :-1], ((0, 0), (1, 0), (0, 0), (0, 0), (0, 0)))
        return jnp.concatenate([prev, t], axis=2)

    qb = gather(q)
    kk = with_prev(gather(k))
    vv = with_prev(gather(v))

    s = jnp.einsum('bnqhd,bnkhd->bnhqk', qb, kk) * (Dh ** -0.5)
    qi = jnp.arange(BLOCK)[:, None] + BLOCK
    kj = jnp.arange(2 * BLOCK)[None, :]
    step = qi - kj
    in_window = (step >= 0) & (step <= steps)
    bucket = _t5_bucket(jnp.maximum(step, 0) * dilation)
    bias = bias_table.astype(jnp.float32)[bucket].transpose(2, 0, 1)
    blk_ok = (jnp.arange(nb)[:, None, None] > 0) | (kj[None] >= BLOCK)
    mask = in_window[None] & blk_ok
    s = jnp.where(mask[None, :, None], s + bias[None, None], -jnp.inf)
    m = jnp.max(s, axis=-1, keepdims=True)
    p = jnp.exp(s - m)
    l = jnp.sum(p, axis=-1, keepdims=True)
    o = jnp.einsum('bnhqk,bnkhd->bnqhd', p / l, vv)
    lse = (m + jnp.log(l))[..., 0].transpose(0, 1, 3, 2)

    o = o.reshape(B * dilation, Lp, H, Dh)[:, :L]
    o = o.reshape(B, dilation, L, H, Dh).transpose(0, 2, 1, 3, 4).reshape(B, S, H, Dh)
    lse = lse.reshape(B * dilation, Lp, H)[:, :L]
    lse = lse.reshape(B, dilation, L, H).transpose(0, 2, 1, 3).reshape(B, S, H)
    return o, lse


def _mixer_a(h, w_in, q_gain, k_gain, w_out, rel_bias):
    B, S, _ = h.shape
    proj = h @ w_in
    qkv = proj[..., :3 * N_GROUPS * ATT_WIDTH].reshape(B, S, N_GROUPS, 3, HEADS_PER_GROUP, HEAD_DIM)
    z = proj[..., 3 * N_GROUPS * ATT_WIDTH:]
    outs, lses = [], []
    for g, (window, dilation) in enumerate(ATT_GROUPS):
        q = _head_rms(qkv[:, :, g, 0], q_gain[g])
        k = _head_rms(qkv[:, :, g, 1], k_gain[g])
        v = qkv[:, :, g, 2].astype(jnp.float32)
        table = rel_bias[:, g * HEADS_PER_GROUP:(g + 1) * HEADS_PER_GROUP]
        o, lse = _dilated_window_attention(q, k, v, table, window, dilation)
        outs.append(o)
        lses.append(lse)
    w = jax.nn.softmax(jnp.stack(lses, axis=0), axis=0)
    o = jnp.sum(w[..., None] * jnp.stack(outs, axis=0), axis=0)
    y = o.reshape(B, S, ATT_WIDTH) * jax.nn.silu(z.astype(jnp.float32))
    return y.astype(h.dtype) @ w_out


def _mixer_b(h, w_in, b_in, conv_w, conv_b, ln_g, ln_b, w_out, b_out):
    proj = h @ w_in + b_in
    a, ga, z = jnp.split(proj, 3, axis=-1)
    u = a * jax.nn.sigmoid(ga)
    u = lax.conv_general_dilated(
        u, conv_w[:, None, :].astype(u.dtype), window_strides=(1,),
        padding=((CONV_WIDTH - 1, 0),), dimension_numbers=('NWC', 'WIO', 'NWC'),
        feature_group_count=CONV_CH) + conv_b
    uf = u.astype(jnp.float32)
    mu = jnp.mean(uf, axis=-1, keepdims=True)
    var = jnp.mean(jnp.square(uf - mu), axis=-1, keepdims=True)
    uf = (uf - mu) * lax.rsqrt(var + EPS) * ln_g.astype(jnp.float32) + ln_b.astype(jnp.float32)
    y = jax.nn.silu(uf) * jax.nn.silu(z.astype(jnp.float32))
    return y.astype(h.dtype) @ w_out + b_out


def setup_inputs(seed: int = 0) -> dict:
    key = jax.random.key(seed)
    ks = jax.random.split(key, 16)
    f32 = jnp.float32
    nrm = lambda k, shape, scale: jax.random.normal(k, shape, f32) * scale
    return {
        "x": nrm(ks[0], (BATCH, SEQ, D_MODEL), 1.0),
        "norm_g": 1.0 + nrm(ks[1], (DEPTH, D_MODEL), 0.02),
        "rel_bias": nrm(ks[2], (N_BUCKETS, N_GROUPS * HEADS_PER_GROUP), 0.3),
        "a_w_in": nrm(ks[3], (N_A, D_MODEL, A_IN_COLS), D_MODEL ** -0.5),
        "a_q_gain": 1.0 + nrm(ks[4], (N_A, N_GROUPS, HEAD_DIM), 0.02),
        "a_k_gain": 1.0 + nrm(ks[5], (N_A, N_GROUPS, HEAD_DIM), 0.02),
        "a_w_out": nrm(ks[6], (N_A, ATT_WIDTH, D_MODEL), ATT_WIDTH ** -0.5),
        "b_w_in": nrm(ks[7], (N_B, D_MODEL, B_IN_COLS), D_MODEL ** -0.5),
        "b_b_in": nrm(ks[8], (N_B, B_IN_COLS), 0.02),
        "b_conv_w": nrm(ks[9], (N_B, CONV_WIDTH, CONV_CH), CONV_WIDTH ** -0.5),
        "b_conv_b": nrm(ks[10], (N_B, CONV_CH), 0.02),
        "b_ln_g": 1.0 + nrm(ks[11], (N_B, CONV_CH), 0.02),
        "b_ln_b": nrm(ks[12], (N_B, CONV_CH), 0.02),
        "b_w_out": nrm(ks[13], (N_B, CONV_CH, D_MODEL), CONV_CH ** -0.5),
        "b_b_out": nrm(ks[14], (N_B, D_MODEL), 0.02),
    }


def reference(x, norm_g, rel_bias, a_w_in, a_q_gain, a_k_gain, a_w_out,
              b_w_in, b_b_in, b_conv_w, b_conv_b, b_ln_g, b_ln_b, b_w_out, b_b_out):
    for i in range(DEPTH):
        h = _rmsnorm(x, norm_g[i])
        j = i // N_MIXERS
        if i % N_MIXERS == 0:
            x = x + _mixer_a(h, a_w_in[j], a_q_gain[j], a_k_gain[j], a_w_out[j], rel_bias)
        else:
            x = x + _mixer_b(h, b_w_in[j], b_b_in[j], b_conv_w[j], b_conv_b[j],
                             b_ln_g[j], b_ln_b[j], b_w_out[j], b_b_out[j])
    return x
```

```python
import functools
import math

import jax
import jax.numpy as jnp
from jax import lax
from jax.experimental import pallas as pl
from jax.experimental.pallas import tpu as pltpu

D_MODEL = 2048
HEAD_DIM = 128
N_HEADS = 16
ATT_GROUPS = ((128, 1), (512, 4), (2048, 16))
N_GROUPS = 3
ATT_WIDTH = N_HEADS * HEAD_DIM
A_COLS = 3 * N_GROUPS * ATT_WIDTH + ATT_WIDTH
BLOCK = 128
N_BUCKETS = 32
MAX_DISTANCE = 2048
CONV_WIDTH = 31
EPS = 1e-6
NEG = -1e30

VMEM_LIMIT_BYTES = 56 * 1024 * 1024
CONV_HALO = 32
CONV_ROWS = 64


def _params(n_axes):
    return pltpu.CompilerParams(
        dimension_semantics=("arbitrary",) * n_axes,
        vmem_limit_bytes=VMEM_LIMIT_BYTES)


def _rmsnorm_to(h_ref, x_ref, g_ref, rows):
    n_chunks = x_ref.shape[0] // rows

    def body(c, carry):
        r0 = pl.multiple_of(c * rows, rows)
        xc = x_ref[pl.ds(r0, rows), :]
        ms = jnp.mean(xc * xc, axis=-1, keepdims=True)
        h_ref[pl.ds(r0, rows), :] = (xc * lax.rsqrt(ms + EPS) * g_ref[...]).astype(h_ref.dtype)
        return carry

    lax.fori_loop(0, n_chunks, body, 0)


def _a_in_kernel(x_ref, g_ref, w_ref, cs_ref, o_ref, h_ref, *, blocks_per_section):
    j = pl.program_id(1)

    @pl.when(j == 0)
    def _():
        _rmsnorm_to(h_ref, x_ref, g_ref, 128)

    acc = jnp.dot(h_ref[...], w_ref[...], preferred_element_type=jnp.float32)
    sec = j // blocks_per_section
    is_qk = jnp.logical_and(sec < 3 * N_GROUPS, sec % 3 != 2)

    @pl.when(is_qk)
    def _():
        for hh in range(acc.shape[1] // HEAD_DIM):
            sl = slice(hh * HEAD_DIM, (hh + 1) * HEAD_DIM)
            a = acc[:, sl]
            ms = jnp.mean(a * a, axis=-1, keepdims=True)
            o_ref[:, sl] = (a * lax.rsqrt(ms + EPS) * cs_ref[:, sl]).astype(o_ref.dtype)

    @pl.when(jnp.logical_not(is_qk))
    def _():
        o_ref[...] = acc.astype(o_ref.dtype)


def _a_in(x2, g, w, colscale, bm=1024, bn=1024):
    m, k = x2.shape
    n = w.shape[1]
    return pl.pallas_call(
        functools.partial(_a_in_kernel, blocks_per_section=ATT_WIDTH // bn),
        grid=(m // bm, n // bn),
        in_specs=[
            pl.BlockSpec((bm, k), lambda i, j: (i, 0)),
            pl.BlockSpec((1, k), lambda i, j: (0, 0)),
            pl.BlockSpec((k, bn), lambda i, j: (0, j)),
            pl.BlockSpec((1, bn), lambda i, j: (0, j)),
        ],
        out_specs=pl.BlockSpec((bm, bn), lambda i, j: (i, j)),
        out_shape=jax.ShapeDtypeStruct((m, n), jnp.bfloat16),
        scratch_shapes=[pltpu.VMEM((bm, k), jnp.bfloat16)],
        compiler_params=_params(2),
        name="a_in_proj",
    )(x2, g, w, colscale)


def _bias_kernel(table_ref, bucket_ref, o_ref):
    hd = pl.program_id(0)
    bucket = bucket_ref[0]
    acc = jnp.full(bucket.shape, NEG, jnp.float32)
    for b in range(N_BUCKETS):
        acc = jnp.where(bucket == b, table_ref[b, hd], acc)
    o_ref[0] = acc


def _expand_bias(rel_bias, bucket):
    n_total = N_GROUPS * N_HEADS
    return pl.pallas_call(
        _bias_kernel,
        grid=(n_total,),
        in_specs=[
            pl.BlockSpec(memory_space=pltpu.SMEM),
            pl.BlockSpec((1, BLOCK, 2 * BLOCK), lambda i: (i // N_HEADS, 0, 0)),
        ],
        out_specs=pl.BlockSpec((1, BLOCK, 2 * BLOCK), lambda i: (i, 0, 0)),
        out_shape=jax.ShapeDtypeStruct((n_total, BLOCK, 2 * BLOCK), jnp.float32),
        compiler_params=_params(1),
        name="rel_bias_expand",
    )(rel_bias, bucket)


def _bucket_tiles():
    max_exact = N_BUCKETS // 2
    qi = jnp.arange(BLOCK)[:, None] + BLOCK
    kj = jnp.arange(2 * BLOCK)[None, :]
    step = qi - kj
    tiles = []
    for window, dilation in ATT_GROUPS:
        steps = window // dilation
        in_window = (step >= 0) & (step <= steps)
        dist = jnp.maximum(step, 0) * dilation
        is_small = dist < max_exact
        ratio = jnp.log(jnp.maximum(dist, 1).astype(jnp.float32) / max_exact) / math.log(MAX_DISTANCE / max_exact)
        large = max_exact + (ratio * (N_BUCKETS - max_exact)).astype(jnp.int32)
        large = jnp.minimum(large, N_BUCKETS - 1)
        bucket = jnp.where(is_small, dist, large)
        tiles.append(jnp.where(in_window, bucket, -1).astype(jnp.int32))
    return jnp.stack(tiles, axis=0)


def _attn_kernel(q_ref, k_ref, v_ref, bias_ref, o_ref, lse_ref, kk_ref, vv_ref):
    n = pl.program_id(2)

    @pl.when(n == 0)
    def _():
        kk_ref[0:BLOCK, :] = jnp.zeros((BLOCK, ATT_WIDTH), kk_ref.dtype)
        vv_ref[0:BLOCK, :] = jnp.zeros((BLOCK, ATT_WIDTH), vv_ref.dtype)

    kk_ref[BLOCK:2 * BLOCK, :] = k_ref[0]
    vv_ref[BLOCK:2 * BLOCK, :] = v_ref[0]

    col = lax.broadcasted_iota(jnp.int32, (1, 2 * BLOCK), 1)
    first = jnp.where(n == 0, NEG, 0.0).astype(jnp.float32)
    prev_mask = jnp.where(col < BLOCK, first, 0.0)

    lane = lax.broadcasted_iota(jnp.int32, (BLOCK, BLOCK), 1)
    lse_tile = jnp.zeros((BLOCK, BLOCK), jnp.float32)
    for hh in range(N_HEADS):
        sl = slice(hh * HEAD_DIM, (hh + 1) * HEAD_DIM)
        q = q_ref[0, :, sl]
        s = lax.dot_general(q, kk_ref[:, sl], (((1,), (1,)), ((), ())),
                            preferred_element_type=jnp.float32)
        s = s + bias_ref[hh] + prev_mask
        m = jnp.max(s, axis=-1, keepdims=True)
        p = jnp.exp(s - m)
        l = jnp.sum(p, axis=-1, keepdims=True)
        o = jnp.dot(p.astype(vv_ref.dtype), vv_ref[:, sl], preferred_element_type=jnp.float32)
        o_ref[0, :, sl] = (o / l).astype(o_ref.dtype)
        lse_tile = jnp.where(lane == hh, m + jnp.log(l), lse_tile)
    lse_ref[0] = lse_tile

    kk_ref[0:BLOCK, :] = k_ref[0]
    vv_ref[0:BLOCK, :] = v_ref[0]


def _attention(proj, bias, g, batch, seq):
    _, d = ATT_GROUPS[g]
    sub_len = seq // d
    nb = sub_len // BLOCK
    sections = A_COLS // ATT_WIDTH
    pv = proj.reshape(batch, sub_len, d * A_COLS)
    base = 3 * g

    def qkv_spec(t):
        return pl.BlockSpec((1, BLOCK, ATT_WIDTH), lambda b, r, n: (b, n, r * sections + base + t))

    o, lse = pl.pallas_call(
        _attn_kernel,
        grid=(batch, d, nb),
        in_specs=[
            qkv_spec(0), qkv_spec(1), qkv_spec(2),
            pl.BlockSpec((N_HEADS, BLOCK, 2 * BLOCK), lambda b, r, n: (g, 0, 0)),
        ],
        out_specs=[
            pl.BlockSpec((1, BLOCK, ATT_WIDTH), lambda b, r, n: (b, n, r)),
            pl.BlockSpec((1, BLOCK, BLOCK), lambda b, r, n: (b, n, r)),
        ],
        out_shape=[
            jax.ShapeDtypeStruct((batch, sub_len, d * ATT_WIDTH), jnp.bfloat16),
            jax.ShapeDtypeStruct((batch, sub_len, d * BLOCK), jnp.float32),
        ],
        scratch_shapes=[
            pltpu.VMEM((2 * BLOCK, ATT_WIDTH), jnp.bfloat16),
            pltpu.VMEM((2 * BLOCK, ATT_WIDTH), jnp.bfloat16),
        ],
        compiler_params=_params(3),
        name=f"dilated_attn_g{g}",
    )(pv, pv, pv, bias)
    return o.reshape(batch * seq, ATT_WIDTH), lse.reshape(batch * seq, BLOCK)


def _a_out_kernel(o0_ref, o1_ref, o2_ref, l0_ref, l1_ref, l2_ref, z_ref, x_ref, w_ref,
                  out_ref, y_ref):
    l0, l1, l2 = l0_ref[...], l1_ref[...], l2_ref[...]
    mx = jnp.maximum(jnp.maximum(l0, l1), l2)
    e0, e1, e2 = jnp.exp(l0 - mx), jnp.exp(l1 - mx), jnp.exp(l2 - mx)
    den = e0 + e1 + e2
    w0, w1, w2 = e0 / den, e1 / den, e2 / den
    for hh in range(N_HEADS):
        sl = slice(hh * HEAD_DIM, (hh + 1) * HEAD_DIM)
        o = (w0[:, hh:hh + 1] * o0_ref[:, sl].astype(jnp.float32)
             + w1[:, hh:hh + 1] * o1_ref[:, sl].astype(jnp.float32)
             + w2[:, hh:hh + 1] * o2_ref[:, sl].astype(jnp.float32))
        y_ref[:, sl] = (o * jax.nn.silu(z_ref[:, sl].astype(jnp.float32))).astype(y_ref.dtype)
    out_ref[...] = x_ref[...] + jnp.dot(y_ref[...], w_ref[...], preferred_element_type=jnp.float32)


def _a_out(os, lses, proj, x2, w_out, bm=256):
    m = x2.shape[0]
    z_block = 3 * N_GROUPS
    row = lambda i: (i, 0)
    return pl.pallas_call(
        _a_out_kernel,
        grid=(m // bm,),
        in_specs=[
            pl.BlockSpec((bm, ATT_WIDTH), row), pl.BlockSpec((bm, ATT_WIDTH), row),
            pl.BlockSpec((bm, ATT_WIDTH), row),
            pl.BlockSpec((bm, BLOCK), row), pl.BlockSpec((bm, BLOCK), row),
            pl.BlockSpec((bm, BLOCK), row),
            pl.BlockSpec((bm, ATT_WIDTH), lambda i: (i, z_block)),
            pl.BlockSpec((bm, D_MODEL), row),
            pl.BlockSpec((ATT_WIDTH, D_MODEL), lambda i: (0, 0)),
        ],
        out_specs=pl.BlockSpec((bm, D_MODEL), row),
        out_shape=jax.ShapeDtypeStruct((m, D_MODEL), jnp.float32),
        scratch_shapes=[pltpu.VMEM((bm, ATT_WIDTH), jnp.bfloat16)],
        compiler_params=_params(1),
        name="a_merge_out_proj",
    )(os[0], os[1], os[2], lses[0], lses[1], lses[2], proj, x2, w_out)


def _b_in_kernel(x_ref, g_ref, wa_ref, wg_ref, wz_ref, ba_ref, bg_ref, bz_ref,
                 u_ref, sz_ref, h_ref):
    @pl.when(pl.program_id(1) == 0)
    def _():
        _rmsnorm_to(h_ref, x_ref, g_ref, 128)

    h = h_ref[...]
    a = jnp.dot(h, wa_ref[...], preferred_element_type=jnp.float32) + ba_ref[...]
    ga = jnp.dot(h, wg_ref[...], preferred_element_type=jnp.float32) + bg_ref[...]
    u_ref[...] = a * jax.nn.sigmoid(ga)
    z = jnp.dot(h, wz_ref[...], preferred_element_type=jnp.float32) + bz_ref[...]
    sz_ref[...] = jax.nn.silu(z).astype(sz_ref.dtype)


def _b_in(x2, g, w, b, bm=1024, bn=512):
    m, k = x2.shape
    ch = w.shape[1] // 3
    nj = ch // bn

    def wspec(t):
        return pl.BlockSpec((k, bn), lambda i, j: (0, t * nj + j))

    def bspec(t):
        return pl.BlockSpec((1, bn), lambda i, j: (0, t * nj + j))

    return pl.pallas_call(
        _b_in_kernel,
        grid=(m // bm, nj),
        in_specs=[
            pl.BlockSpec((bm, k), lambda i, j: (i, 0)),
            pl.BlockSpec((1, k), lambda i, j: (0, 0)),
            wspec(0), wspec(1), wspec(2), bspec(0), bspec(1), bspec(2),
        ],
        out_specs=[
            pl.BlockSpec((bm, bn), lambda i, j: (i, j)),
            pl.BlockSpec((bm, bn), lambda i, j: (i, j)),
        ],
        out_shape=[
            jax.ShapeDtypeStruct((m, ch), jnp.float32),
            jax.ShapeDtypeStruct((m, ch), jnp.bfloat16),
        ],
        scratch_shapes=[pltpu.VMEM((bm, k), jnp.bfloat16)],
        compiler_params=_params(2),
        name="b_in_proj_glu",
    )(x2, g, w, w, w, b, b, b)


def _b_out_kernel(u_ref, halo_ref, sz_ref, x_ref, cw_ref, cb_ref, lg_ref, lb_ref,
                  w_ref, bo_ref, out_ref, ext_ref, conv_ref, y_ref, *, blocks_per_seq):
    bm = u_ref.shape[0]
    n_slabs = u_ref.shape[1] // 128
    i = pl.program_id(0)
    seq_start = (i % blocks_per_seq) == 0

    for c in range(n_slabs):
        sl = slice(c * 128, (c + 1) * 128)
        ext_ref[c, CONV_HALO:CONV_HALO + bm, :] = u_ref[:, sl]

    @pl.when(seq_start)
    def _():
        ext_ref[:, 0:CONV_HALO, :] = jnp.zeros((n_slabs, CONV_HALO, 128), jnp.float32)

    @pl.when(jnp.logical_not(seq_start))
    def _():
        for c in range(n_slabs):
            ext_ref[c, 0:CONV_HALO, :] = halo_ref[:, c * 128:(c + 1) * 128]

    first_tap = CONV_HALO - (CONV_WIDTH - 1)
    chunks = bm // CONV_ROWS

    def conv_body(t, carry):
        c = t // chunks
        r0 = pl.multiple_of((t % chunks) * CONV_ROWS, CONV_ROWS)
        acc = jnp.zeros((CONV_ROWS, 128), jnp.float32)
        for kk in range(CONV_WIDTH):
            acc = acc + cw_ref[c, kk:kk + 1, :] * ext_ref[c, pl.ds(r0 + first_tap + kk, CONV_ROWS), :]
        conv_ref[c, pl.ds(r0, CONV_ROWS), :] = acc
        return carry

    lax.fori_loop(0, n_slabs * chunks, conv_body, 0)

    tot = jnp.zeros((bm, 128), jnp.float32)
    for c in range(n_slabs):
        sl = slice(c * 128, (c + 1) * 128)
        tot = tot + (conv_ref[c] + cb_ref[:, sl])
    mu = jnp.sum(tot, axis=-1, keepdims=True) / (n_slabs * 128)
    sq = jnp.zeros((bm, 128), jnp.float32)
    for c in range(n_slabs):
        sl = slice(c * 128, (c + 1) * 128)
        dv = conv_ref[c] + cb_ref[:, sl] - mu
        sq = sq + dv * dv
    var = jnp.sum(sq, axis=-1, keepdims=True) / (n_slabs * 128)
    inv = lax.rsqrt(var + EPS)
    for c in range(n_slabs):
        sl = slice(c * 128, (c + 1) * 128)
        uf = (conv_ref[c] + cb_ref[:, sl] - mu) * inv * lg_ref[:, sl] + lb_ref[:, sl]
        y_ref[:, sl] = (jax.nn.silu(uf) * sz_ref[:, sl].astype(jnp.float32)).astype(y_ref.dtype)

    out_ref[...] = (x_ref[...] + bo_ref[...]
                    + jnp.dot(y_ref[...], w_ref[...], preferred_element_type=jnp.float32))


def _b_out(u, sz, x2, conv_w_slabs, conv_b, ln_g, ln_b, w_out, b_out, seq, bm=256):
    m, ch = u.shape
    n_slabs = ch // 128
    halo_blocks = bm // CONV_HALO
    row = lambda i: (i, 0)
    const2 = lambda i: (0, 0)
    return pl.pallas_call(
        functools.partial(_b_out_kernel, blocks_per_seq=seq // bm),
        grid=(m // bm,),
        in_specs=[
            pl.BlockSpec((bm, ch), row),
            pl.BlockSpec((CONV_HALO, ch), lambda i: (jnp.maximum(i * halo_blocks - 1, 0), 0)),
            pl.BlockSpec((bm, ch), row),
            pl.BlockSpec((bm, D_MODEL), row),
            pl.BlockSpec((n_slabs, CONV_HALO, 128), lambda i: (0, 0, 0)),
            pl.BlockSpec((1, ch), const2), pl.BlockSpec((1, ch), const2),
            pl.BlockSpec((1, ch), const2),
            pl.BlockSpec((ch, D_MODEL), const2),
            pl.BlockSpec((1, D_MODEL), const2),
        ],
        out_specs=pl.BlockSpec((bm, D_MODEL), row),
        out_shape=jax.ShapeDtypeStruct((m, D_MODEL), jnp.float32),
        scratch_shapes=[
            pltpu.VMEM((n_slabs, CONV_HALO + bm, 128), jnp.float32),
            pltpu.VMEM((n_slabs, bm, 128), jnp.float32),
            pltpu.VMEM((bm, ch), jnp.bfloat16),
        ],
        compiler_params=_params(1),
        name="b_conv_ln_out_proj",
    )(u, u, sz, x2, conv_w_slabs, conv_b, ln_g, ln_b, w_out, b_out)


def kernel(x, norm_g, rel_bias, a_w_in, a_q_gain, a_k_gain, a_w_out, b_w_in, b_b_in,
           b_conv_w, b_conv_b, b_ln_g, b_ln_b, b_w_out, b_b_out):
    batch, seq, dm = x.shape
    m = batch * seq
    bf16 = jnp.bfloat16
    x2 = x.reshape(m, dm)

    ones = jnp.ones((ATT_WIDTH,), jnp.float32)
    cols = []
    for g in range(N_GROUPS):
        cols += [jnp.tile(a_q_gain[0, g], N_HEADS) * (HEAD_DIM ** -0.5),
                 jnp.tile(a_k_gain[0, g], N_HEADS), ones]
    cols.append(ones)
    colscale = jnp.concatenate(cols).reshape(1, A_COLS)

    proj = _a_in(x2, norm_g[0:1], a_w_in[0].astype(bf16), colscale)
    bias = _expand_bias(rel_bias, _bucket_tiles())
    os, lses = [], []
    for g in range(N_GROUPS):
        o, lse = _attention(proj, bias, g, batch, seq)
        os.append(o)
        lses.append(lse)
    x2 = _a_out(os, lses, proj, x2, a_w_out[0].astype(bf16))

    u, sz = _b_in(x2, norm_g[1:2], b_w_in[0].astype(bf16), b_b_in[0:1])
    ch = u.shape[1]
    cw = jnp.pad(b_conv_w[0], ((0, CONV_HALO - CONV_WIDTH), (0, 0)))
    cw = cw.reshape(CONV_HALO, ch // 128, 128).transpose(1, 0, 2)
    x2 = _b_out(u, sz, x2, cw, b_conv_b[0:1], b_ln_g[0:1], b_ln_b[0:1],
                b_w_out[0].astype(bf16), b_b_out[0:1], seq)
    return x2.reshape(batch, seq, dm)
```

```python
import functools
import math

import jax
import jax.numpy as jnp
from jax import lax
from jax.experimental import pallas as pl
from jax.experimental.pallas import tpu as pltpu

D_MODEL = 2048
HEAD_DIM = 128
N_HEADS = 16
ATT_GROUPS = ((128, 1), (512, 4), (2048, 16))
N_GROUPS = 3
ATT_WIDTH = N_HEADS * HEAD_DIM
A_COLS = 3 * N_GROUPS * ATT_WIDTH + ATT_WIDTH
BLOCK = 128
N_BUCKETS = 32
MAX_DISTANCE = 2048
CONV_WIDTH = 31
EPS = 1e-6
NEG = -1e30

VMEM_LIMIT_BYTES = 56 * 1024 * 1024
CONV_HALO = 32
CONV_ROWS = 64


def _params(n_axes):
    return pltpu.CompilerParams(
        dimension_semantics=("arbitrary",) * n_axes,
        vmem_limit_bytes=VMEM_LIMIT_BYTES)


def _rmsnorm_to(h_ref, x_ref, g_ref, rows):
    n_chunks = x_ref.shape[0] // rows

    def body(c, carry):
        r0 = pl.multiple_of(c * rows, rows)
        xc = x_ref[pl.ds(r0, rows), :]
        ms = jnp.mean(xc * xc, axis=-1, keepdims=True)
        h_ref[pl.ds(r0, rows), :] = (xc * lax.rsqrt(ms + EPS) * g_ref[...]).astype(h_ref.dtype)
        return carry

    lax.fori_loop(0, n_chunks, body, 0)


def _a_norm_kernel(x_ref, g_ref, hn_ref, h4_ref, h16_ref, slab_ref):
    bm = x_ref.shape[0]
    n_slabs = x_ref.shape[1] // 128
    rows = 128
    for c0 in range(bm // rows):
        rs = slice(c0 * rows, (c0 + 1) * rows)
        xc = x_ref[rs, :]
        ms = jnp.mean(xc * xc, axis=-1, keepdims=True)
        hn = xc * lax.rsqrt(ms + EPS) * g_ref[...]
        hn_ref[rs, :] = hn.astype(hn_ref.dtype)
        for c in range(n_slabs):
            slab_ref[c, rs, :] = hn[:, c * 128:(c + 1) * 128]
    for c in range(n_slabs):
        sl = slice(c * 128, (c + 1) * 128)
        for r in range(4):
            h4_ref[0, r, :, sl] = slab_ref[c, pl.ds(r, bm // 4, stride=4), :].astype(h4_ref.dtype)
        for r in range(16):
            h16_ref[0, r, :, sl] = slab_ref[c, pl.ds(r, bm // 16, stride=16), :].astype(h16_ref.dtype)


def _a_norm(x2, g, batch, seq, bm=512):
    m, k = x2.shape
    per_seq = seq // bm
    bf16 = jnp.bfloat16

    def perm_spec(d):
        return pl.BlockSpec((1, d, bm // d, k), lambda i: (i // per_seq, 0, i % per_seq, 0))

    return pl.pallas_call(
        _a_norm_kernel,
        grid=(m // bm,),
        in_specs=[pl.BlockSpec((bm, k), lambda i: (i, 0)), pl.BlockSpec((1, k), lambda i: (0, 0))],
        out_specs=[pl.BlockSpec((bm, k), lambda i: (i, 0)), perm_spec(4), perm_spec(16)],
        out_shape=[
            jax.ShapeDtypeStruct((m, k), bf16),
            jax.ShapeDtypeStruct((batch, 4, seq // 4, k), bf16),
            jax.ShapeDtypeStruct((batch, 16, seq // 16, k), bf16),
        ],
        scratch_shapes=[pltpu.VMEM((k // 128, bm, 128), jnp.float32)],
        compiler_params=_params(1),
        name="a_rmsnorm_permute",
    )(x2, g)


def _a_in_kernel(h_ref, w_ref, cs_ref, o_ref, *, blocks_per_section):
    acc = jnp.dot(h_ref[...], w_ref[...], preferred_element_type=jnp.float32)
    is_qk = (pl.program_id(1) // blocks_per_section) < 2

    @pl.when(is_qk)
    def _():
        for hh in range(acc.shape[1] // HEAD_DIM):
            sl = slice(hh * HEAD_DIM, (hh + 1) * HEAD_DIM)
            a = acc[:, sl]
            ms = jnp.mean(a * a, axis=-1, keepdims=True)
            o_ref[:, sl] = (a * lax.rsqrt(ms + EPS) * cs_ref[:, sl]).astype(o_ref.dtype)

    @pl.when(jnp.logical_not(is_qk))
    def _():
        o_ref[...] = acc.astype(o_ref.dtype)


def _a_in(h, w, colscale, g, with_gate, bm=1024, bn=1024):
    m, k = h.shape
    per_section = ATT_WIDTH // bn
    qkv_blocks = 3 * per_section
    n_blocks = qkv_blocks + (per_section if with_gate else 0)
    gate_shift = (3 * N_GROUPS - 3) * per_section

    def col(i, j):
        return (0, jnp.where(j < qkv_blocks, g * qkv_blocks + j, j + gate_shift))

    return pl.pallas_call(
        functools.partial(_a_in_kernel, blocks_per_section=per_section),
        grid=(m // bm, n_blocks),
        in_specs=[
            pl.BlockSpec((bm, k), lambda i, j: (i, 0)),
            pl.BlockSpec((k, bn), col),
            pl.BlockSpec((1, bn), col),
        ],
        out_specs=pl.BlockSpec((bm, bn), lambda i, j: (i, j)),
        out_shape=jax.ShapeDtypeStruct((m, n_blocks * bn), jnp.bfloat16),
        compiler_params=_params(2),
        name=f"a_in_proj_g{g}",
    )(h, w, colscale)


def _bias_kernel(table_ref, bucket_ref, o_ref):
    hd = pl.program_id(0)
    bucket = bucket_ref[0]
    acc = jnp.full(bucket.shape, NEG, jnp.float32)
    for b in range(N_BUCKETS):
        acc = jnp.where(bucket == b, table_ref[b, hd], acc)
    o_ref[0] = acc


def _expand_bias(rel_bias, bucket):
    n_total = N_GROUPS * N_HEADS
    return pl.pallas_call(
        _bias_kernel,
        grid=(n_total,),
        in_specs=[
            pl.BlockSpec(memory_space=pltpu.SMEM),
            pl.BlockSpec((1, BLOCK, 2 * BLOCK), lambda i: (i // N_HEADS, 0, 0)),
        ],
        out_specs=pl.BlockSpec((1, BLOCK, 2 * BLOCK), lambda i: (i, 0, 0)),
        out_shape=jax.ShapeDtypeStruct((n_total, BLOCK, 2 * BLOCK), jnp.float32),
        compiler_params=_params(1),
        name="rel_bias_expand",
    )(rel_bias, bucket)


def _bucket_tiles():
    max_exact = N_BUCKETS // 2
    qi = jnp.arange(BLOCK)[:, None] + BLOCK
    kj = jnp.arange(2 * BLOCK)[None, :]
    step = qi - kj
    tiles = []
    for window, dilation in ATT_GROUPS:
        steps = window // dilation
        in_window = (step >= 0) & (step <= steps)
        dist = jnp.maximum(step, 0) * dilation
        is_small = dist < max_exact
        ratio = jnp.log(jnp.maximum(dist, 1).astype(jnp.float32) / max_exact) / math.log(MAX_DISTANCE / max_exact)
        large = max_exact + (ratio * (N_BUCKETS - max_exact)).astype(jnp.int32)
        large = jnp.minimum(large, N_BUCKETS - 1)
        bucket = jnp.where(is_small, dist, large)
        tiles.append(jnp.where(in_window, bucket, -1).astype(jnp.int32))
    return jnp.stack(tiles, axis=0)


def _attn_kernel(q_ref, k_ref, v_ref, bias_ref, o_ref, lse_ref, kk_ref, vv_ref):
    n = pl.program_id(1)

    @pl.when(n == 0)
    def _():
        kk_ref[0:BLOCK, :] = jnp.zeros((BLOCK, ATT_WIDTH), kk_ref.dtype)
        vv_ref[0:BLOCK, :] = jnp.zeros((BLOCK, ATT_WIDTH), vv_ref.dtype)

    kk_ref[BLOCK:2 * BLOCK, :] = k_ref[0]
    vv_ref[BLOCK:2 * BLOCK, :] = v_ref[0]

    col = lax.broadcasted_iota(jnp.int32, (1, 2 * BLOCK), 1)
    first = jnp.where(n == 0, NEG, 0.0).astype(jnp.float32)
    prev_mask = jnp.where(col < BLOCK, first, 0.0)

    lane = lax.broadcasted_iota(jnp.int32, (BLOCK, BLOCK), 1)
    lse_tile = jnp.zeros((BLOCK, BLOCK), jnp.float32)
    for hh in range(N_HEADS):
        sl = slice(hh * HEAD_DIM, (hh + 1) * HEAD_DIM)
        q = q_ref[0, :, sl]
        s = lax.dot_general(q, kk_ref[:, sl], (((1,), (1,)), ((), ())),
                            preferred_element_type=jnp.float32)
        s = s + bias_ref[hh] + prev_mask
        m = jnp.max(s, axis=-1, keepdims=True)
        p = jnp.exp(s - m)
        l = jnp.sum(p, axis=-1, keepdims=True)
        o = jnp.dot(p.astype(vv_ref.dtype), vv_ref[:, sl], preferred_element_type=jnp.float32)
        o_ref[0, :, sl] = (o / l).astype(o_ref.dtype)
        lse_tile = jnp.where(lane == hh, m + jnp.log(l), lse_tile)
    lse_ref[0] = lse_tile

    kk_ref[0:BLOCK, :] = k_ref[0]
    vv_ref[0:BLOCK, :] = v_ref[0]


def _attention(qkv, bias, g, batch, seq):
    _, d = ATT_GROUPS[g]
    sub_len = seq // d
    nb = sub_len // BLOCK
    qkv3 = qkv.reshape(batch * d, sub_len, qkv.shape[1])

    def qkv_spec(t):
        return pl.BlockSpec((1, BLOCK, ATT_WIDTH), lambda s, n: (s, n, t))

    return pl.pallas_call(
        _attn_kernel,
        grid=(batch * d, nb),
        in_specs=[
            qkv_spec(0), qkv_spec(1), qkv_spec(2),
            pl.BlockSpec((N_HEADS, BLOCK, 2 * BLOCK), lambda s, n: (g, 0, 0)),
        ],
        out_specs=[
            pl.BlockSpec((1, BLOCK, ATT_WIDTH), lambda s, n: (s, n, 0)),
            pl.BlockSpec((1, BLOCK, BLOCK), lambda s, n: (s, n, 0)),
        ],
        out_shape=[
            jax.ShapeDtypeStruct((batch * d, sub_len, ATT_WIDTH), jnp.bfloat16),
            jax.ShapeDtypeStruct((batch * d, sub_len, BLOCK), jnp.float32),
        ],
        scratch_shapes=[
            pltpu.VMEM((2 * BLOCK, ATT_WIDTH), jnp.bfloat16),
            pltpu.VMEM((2 * BLOCK, ATT_WIDTH), jnp.bfloat16),
        ],
        compiler_params=_params(2),
        name=f"dilated_attn_g{g}",
    )(qkv3, qkv3, qkv3, bias)


def _a_out_kernel(o0_ref, o1_ref, o2_ref, l0_ref, l1_ref, l2_ref, z_ref, x_ref, w_ref,
                  out_ref, l1n_ref, l2n_ref, t1_ref, t2_ref, y_ref):
    bm = x_ref.shape[0]
    for r in range(4):
        l1n_ref[pl.ds(r, bm // 4, stride=4), :] = l1_ref[0, r]
    for r in range(16):
        l2n_ref[pl.ds(r, bm // 16, stride=16), :] = l2_ref[0, r]
    l0, l1, l2 = l0_ref[...], l1n_ref[...], l2n_ref[...]
    mx = jnp.maximum(jnp.maximum(l0, l1), l2)
    e0, e1, e2 = jnp.exp(l0 - mx), jnp.exp(l1 - mx), jnp.exp(l2 - mx)
    den = e0 + e1 + e2
    w0, w1, w2 = e0 / den, e1 / den, e2 / den
    for hh in range(N_HEADS):
        sl = slice(hh * HEAD_DIM, (hh + 1) * HEAD_DIM)
        for r in range(4):
            t1_ref[hh, pl.ds(r, bm // 4, stride=4), :] = o1_ref[0, r, :, sl].astype(jnp.float32)
        for r in range(16):
            t2_ref[hh, pl.ds(r, bm // 16, stride=16), :] = o2_ref[0, r, :, sl].astype(jnp.float32)
        o = (w0[:, hh:hh + 1] * o0_ref[:, sl].astype(jnp.float32)
             + w1[:, hh:hh + 1] * t1_ref[hh]
             + w2[:, hh:hh + 1] * t2_ref[hh])
        y_ref[:, sl] = (o * jax.nn.silu(z_ref[:, sl].astype(jnp.float32))).astype(y_ref.dtype)
    out_ref[...] = x_ref[...] + jnp.dot(y_ref[...], w_ref[...], preferred_element_type=jnp.float32)


def _a_out(os, lses, p0, x2, w_out, batch, seq, bm=256):
    m = x2.shape[0]
    per_seq = seq // bm
    z_block = 3
    row = lambda i: (i, 0)

    def perm_spec(d, width):
        return pl.BlockSpec((1, d, bm // d, width), lambda i: (i // per_seq, 0, i % per_seq, 0))

    def perm_view(a, d):
        return a.reshape(batch, d, seq // d, a.shape[-1])

    return pl.pallas_call(
        _a_out_kernel,
        grid=(m // bm,),
        in_specs=[
            pl.BlockSpec((bm, ATT_WIDTH), row), perm_spec(4, ATT_WIDTH), perm_spec(16, ATT_WIDTH),
            pl.BlockSpec((bm, BLOCK), row), perm_spec(4, BLOCK), perm_spec(16, BLOCK),
            pl.BlockSpec((bm, ATT_WIDTH), lambda i: (i, z_block)),
            pl.BlockSpec((bm, D_MODEL), row),
            pl.BlockSpec((ATT_WIDTH, D_MODEL), lambda i: (0, 0)),
        ],
        out_specs=pl.BlockSpec((bm, D_MODEL), row),
        out_shape=jax.ShapeDtypeStruct((m, D_MODEL), jnp.float32),
        scratch_shapes=[
            pltpu.VMEM((bm, BLOCK), jnp.float32),
            pltpu.VMEM((bm, BLOCK), jnp.float32),
            pltpu.VMEM((N_HEADS, bm, HEAD_DIM), jnp.float32),
            pltpu.VMEM((N_HEADS, bm, HEAD_DIM), jnp.float32),
            pltpu.VMEM((bm, ATT_WIDTH), jnp.bfloat16),
        ],
        compiler_params=_params(1),
        name="a_merge_out_proj",
    )(os[0].reshape(m, ATT_WIDTH), perm_view(os[1], 4), perm_view(os[2], 16),
      lses[0].reshape(m, BLOCK), perm_view(lses[1], 4), perm_view(lses[2], 16),
      p0, x2, w_out)


def _b_in_kernel(x_ref, g_ref, wa_ref, wg_ref, wz_ref, ba_ref, bg_ref, bz_ref,
                 u_ref, sz_ref, h_ref):
    @pl.when(pl.program_id(1) == 0)
    def _():
        _rmsnorm_to(h_ref, x_ref, g_ref, 128)

    h = h_ref[...]
    a = jnp.dot(h, wa_ref[...], preferred_element_type=jnp.float32) + ba_ref[...]
    ga = jnp.dot(h, wg_ref[...], preferred_element_type=jnp.float32) + bg_ref[...]
    u_ref[...] = a * jax.nn.sigmoid(ga)
    z = jnp.dot(h, wz_ref[...], preferred_element_type=jnp.float32) + bz_ref[...]
    sz_ref[...] = jax.nn.silu(z).astype(sz_ref.dtype)


def _b_in(x2, g, w, b, bm=1024, bn=512):
    m, k = x2.shape
    ch = w.shape[1] // 3
    nj = ch // bn

    def wspec(t):
        return pl.BlockSpec((k, bn), lambda i, j: (0, t * nj + j))

    def bspec(t):
        return pl.BlockSpec((1, bn), lambda i, j: (0, t * nj + j))

    return pl.pallas_call(
        _b_in_kernel,
        grid=(m // bm, nj),
        in_specs=[
            pl.BlockSpec((bm, k), lambda i, j: (i, 0)),
            pl.BlockSpec((1, k), lambda i, j: (0, 0)),
            wspec(0), wspec(1), wspec(2), bspec(0), bspec(1), bspec(2),
        ],
        out_specs=[
            pl.BlockSpec((bm, bn), lambda i, j: (i, j)),
            pl.BlockSpec((bm, bn), lambda i, j: (i, j)),
        ],
        out_shape=[
            jax.ShapeDtypeStruct((m, ch), jnp.float32),
            jax.ShapeDtypeStruct((m, ch), jnp.bfloat16),
        ],
        scratch_shapes=[pltpu.VMEM((bm, k), jnp.bfloat16)],
        compiler_params=_params(2),
        name="b_in_proj_glu",
    )(x2, g, w, w, w, b, b, b)


def _b_out_kernel(u_ref, halo_ref, sz_ref, x_ref, cw_ref, cb_ref, lg_ref, lb_ref,
                  w_ref, bo_ref, out_ref, ext_ref, conv_ref, y_ref, *, blocks_per_seq):
    bm = u_ref.shape[0]
    n_slabs = u_ref.shape[1] // 128
    i = pl.program_id(0)
    seq_start = (i % blocks_per_seq) == 0

    for c in range(n_slabs):
        sl = slice(c * 128, (c + 1) * 128)
        ext_ref[c, CONV_HALO:CONV_HALO + bm, :] = u_ref[:, sl]

    @pl.when(seq_start)
    def _():
        ext_ref[:, 0:CONV_HALO, :] = jnp.zeros((n_slabs, CONV_HALO, 128), jnp.float32)

    @pl.when(jnp.logical_not(seq_start))
    def _():
        for c in range(n_slabs):
            ext_ref[c, 0:CONV_HALO, :] = halo_ref[:, c * 128:(c + 1) * 128]

    first_tap = CONV_HALO - (CONV_WIDTH - 1)
    chunks = bm // CONV_ROWS

    def conv_body(t, carry):
        c = t // chunks
        r0 = pl.multiple_of((t % chunks) * CONV_ROWS, CONV_ROWS)
        acc = jnp.zeros((CONV_ROWS, 128), jnp.float32)
        for kk in range(CONV_WIDTH):
            acc = acc + cw_ref[c, kk:kk + 1, :] * ext_ref[c, pl.ds(r0 + first_tap + kk, CONV_ROWS), :]
        conv_ref[c, pl.ds(r0, CONV_ROWS), :] = acc
        return carry

    lax.fori_loop(0, n_slabs * chunks, conv_body, 0)

    tot = jnp.zeros((bm, 128), jnp.float32)
    for c in range(n_slabs):
        sl = slice(c * 128, (c + 1) * 128)
        tot = tot + (conv_ref[c] + cb_ref[:, sl])
    mu = jnp.sum(tot, axis=-1, keepdims=True) / (n_slabs * 128)
    sq = jnp.zeros((bm, 128), jnp.float32)
    for c in range(n_slabs):
        sl = slice(c * 128, (c + 1) * 128)
        dv = conv_ref[c] + cb_ref[:, sl] - mu
        sq = sq + dv * dv
    var = jnp.sum(sq, axis=-1, keepdims=True) / (n_slabs * 128)
    inv = lax.rsqrt(var + EPS)
    for c in range(n_slabs):
        sl = slice(c * 128, (c + 1) * 128)
        uf = (conv_ref[c] + cb_ref[:, sl] - mu) * inv * lg_ref[:, sl] + lb_ref[:, sl]
        y_ref[:, sl] = (jax.nn.silu(uf) * sz_ref[:, sl].astype(jnp.float32)).astype(y_ref.dtype)

    out_ref[...] = (x_ref[...] + bo_ref[...]
                    + jnp.dot(y_ref[...], w_ref[...], preferred_element_type=jnp.float32))


def _b_out(u, sz, x2, conv_w_slabs, conv_b, ln_g, ln_b, w_out, b_out, seq, bm=256):
    m, ch = u.shape
    n_slabs = ch // 128
    halo_blocks = bm // CONV_HALO
    row = lambda i: (i, 0)
    const2 = lambda i: (0, 0)
    return pl.pallas_call(
        functools.partial(_b_out_kernel, blocks_per_seq=seq // bm),
        grid=(m // bm,),
        in_specs=[
            pl.BlockSpec((bm, ch), row),
            pl.BlockSpec((CONV_HALO, ch), lambda i: (jnp.maximum(i * halo_blocks - 1, 0), 0)),
            pl.BlockSpec((bm, ch), row),
            pl.BlockSpec((bm, D_MODEL), row),
            pl.BlockSpec((n_slabs, CONV_HALO, 128), lambda i: (0, 0, 0)),
            pl.BlockSpec((1, ch), const2), pl.BlockSpec((1, ch), const2),
            pl.BlockSpec((1, ch), const2),
            pl.BlockSpec((ch, D_MODEL), const2),
            pl.BlockSpec((1, D_MODEL), const2),
        ],
        out_specs=pl.BlockSpec((bm, D_MODEL), row),
        out_shape=jax.ShapeDtypeStruct((m, D_MODEL), jnp.float32),
        scratch_shapes=[
            pltpu.VMEM((n_slabs, CONV_HALO + bm, 128), jnp.float32),
            pltpu.VMEM((n_slabs, bm, 128), jnp.float32),
            pltpu.VMEM((bm, ch), jnp.bfloat16),
        ],
        compiler_params=_params(1),
        name="b_conv_ln_out_proj",
    )(u, u, sz, x2, conv_w_slabs, conv_b, ln_g, ln_b, w_out, b_out)


def kernel(x, norm_g, rel_bias, a_w_in, a_q_gain, a_k_gain, a_w_out, b_w_in, b_b_in,
           b_conv_w, b_conv_b, b_ln_g, b_ln_b, b_w_out, b_b_out):
    batch, seq, dm = x.shape
    m = batch * seq
    bf16 = jnp.bfloat16
    x2 = x.reshape(m, dm)

    ones = jnp.ones((ATT_WIDTH,), jnp.float32)
    cols = []
    for g in range(N_GROUPS):
        cols += [jnp.tile(a_q_gain[0, g], N_HEADS) * (HEAD_DIM ** -0.5),
                 jnp.tile(a_k_gain[0, g], N_HEADS), ones]
    cols.append(ones)
    colscale = jnp.concatenate(cols).reshape(1, A_COLS)

    w_in = a_w_in[0].astype(bf16)
    hs = _a_norm(x2, norm_g[0:1], batch, seq)
    bias = _expand_bias(rel_bias, _bucket_tiles())
    os, lses, p0 = [], [], None
    for g in range(N_GROUPS):
        pg = _a_in(hs[g].reshape(m, dm), w_in, colscale, g, with_gate=(g == 0))
        if g == 0:
            p0 = pg
        o, lse = _attention(pg, bias, g, batch, seq)
        os.append(o)
        lses.append(lse)
    x2 = _a_out(os, lses, p0, x2, a_w_out[0].astype(bf16), batch, seq)

    u, sz = _b_in(x2, norm_g[1:2], b_w_in[0].astype(bf16), b_b_in[0:1])
    ch = u.shape[1]
    cw = jnp.pad(b_conv_w[0], ((0, CONV_HALO - CONV_WIDTH), (0, 0)))
    cw = cw.reshape(CONV_HALO, ch // 128, 128).transpose(1, 0, 2)
    x2 = _b_out(u, sz, x2, cw, b_conv_b[0:1], b_ln_g[0:1], b_ln_b[0:1],
                b_w_out[0].astype(bf16), b_b_out[0:1], seq)
    return x2.reshape(batch, seq, dm)
```

```python
import functools
import math

import jax
import jax.numpy as jnp
from jax import lax
from jax.experimental import pallas as pl
from jax.experimental.pallas import tpu as pltpu

D_MODEL = 2048
HEAD_DIM = 128
N_HEADS = 16
ATT_GROUPS = ((128, 1), (512, 4), (2048, 16))
N_GROUPS = 3
ATT_WIDTH = N_HEADS * HEAD_DIM
A_COLS = 3 * N_GROUPS * ATT_WIDTH + ATT_WIDTH
BLOCK = 128
N_BUCKETS = 32
MAX_DISTANCE = 2048
CONV_WIDTH = 31
EPS = 1e-6
NEG = -1e30
LOG2E = math.log2(math.e)
LN2 = math.log(2.0)

VMEM_LIMIT_BYTES = 56 * 1024 * 1024
CONV_HALO = 32
CONV_ROWS = 64
K_CHUNK_HEADS = 2


def _params(n_axes):
    return pltpu.CompilerParams(
        dimension_semantics=("arbitrary",) * n_axes,
        vmem_limit_bytes=VMEM_LIMIT_BYTES)


def _rmsnorm_to(h_ref, x_ref, g_ref, rows):
    n_chunks = x_ref.shape[0] // rows

    def body(c, carry):
        r0 = pl.multiple_of(c * rows, rows)
        xc = x_ref[pl.ds(r0, rows), :]
        ms = jnp.mean(xc * xc, axis=-1, keepdims=True)
        h_ref[pl.ds(r0, rows), :] = (xc * lax.rsqrt(ms + EPS) * g_ref[...]).astype(h_ref.dtype)
        return carry

    lax.fori_loop(0, n_chunks, body, 0)


def _a_norm_kernel(x_ref, g_ref, hn_ref, h4_ref, h16_ref, slab_ref):
    bm = x_ref.shape[0]
    n_slabs = x_ref.shape[1] // 128
    rows = 128
    for c0 in range(bm // rows):
        rs = slice(c0 * rows, (c0 + 1) * rows)
        xc = x_ref[rs, :]
        ms = jnp.mean(xc * xc, axis=-1, keepdims=True)
        hn = xc * lax.rsqrt(ms + EPS) * g_ref[...]
        hn_ref[rs, :] = hn.astype(hn_ref.dtype)
        for c in range(n_slabs):
            slab_ref[c, rs, :] = hn[:, c * 128:(c + 1) * 128]
    for c in range(n_slabs):
        sl = slice(c * 128, (c + 1) * 128)
        for r in range(4):
            h4_ref[0, r, :, sl] = slab_ref[c, pl.ds(r, bm // 4, stride=4), :].astype(h4_ref.dtype)
        for r in range(16):
            h16_ref[0, r, :, sl] = slab_ref[c, pl.ds(r, bm // 16, stride=16), :].astype(h16_ref.dtype)


def _a_norm(x2, g, batch, seq, bm=512):
    m, k = x2.shape
    per_seq = seq // bm
    bf16 = jnp.bfloat16

    def perm_spec(d):
        return pl.BlockSpec((1, d, bm // d, k), lambda i: (i // per_seq, 0, i % per_seq, 0))

    return pl.pallas_call(
        _a_norm_kernel,
        grid=(m // bm,),
        in_specs=[pl.BlockSpec((bm, k), lambda i: (i, 0)), pl.BlockSpec((1, k), lambda i: (0, 0))],
        out_specs=[pl.BlockSpec((bm, k), lambda i: (i, 0)), perm_spec(4), perm_spec(16)],
        out_shape=[
            jax.ShapeDtypeStruct((m, k), bf16),
            jax.ShapeDtypeStruct((batch, 4, seq // 4, k), bf16),
            jax.ShapeDtypeStruct((batch, 16, seq // 16, k), bf16),
        ],
        scratch_shapes=[pltpu.VMEM((k // 128, bm, 128), jnp.float32)],
        compiler_params=_params(1),
        name="a_rmsnorm_permute",
    )(x2, g)


def _a_in_kernel(h_ref, w_ref, cs_ref, o_ref, wb_ref, *, blocks_per_section):
    @pl.when(pl.program_id(1) == 0)
    def _():
        rows = 256
        for c0 in range(w_ref.shape[0] // rows):
            rs = slice(c0 * rows, (c0 + 1) * rows)
            wb_ref[rs, :] = w_ref[rs, :].astype(wb_ref.dtype)

    acc = jnp.dot(h_ref[...], wb_ref[...], preferred_element_type=jnp.float32)
    is_qk = (pl.program_id(0) // blocks_per_section) < 2
    for hh in range(acc.shape[1] // HEAD_DIM):
        sl = slice(hh * HEAD_DIM, (hh + 1) * HEAD_DIM)
        a = acc[:, sl]
        ms = jnp.mean(a * a, axis=-1, keepdims=True)
        scale = jnp.where(is_qk, lax.rsqrt(ms + EPS) * cs_ref[:, sl], 1.0)
        o_ref[:, sl] = (a * scale).astype(o_ref.dtype)


def _a_in(h, w, colscale, g, with_gate, bm=1024, bn=1024):
    m, k = h.shape
    per_section = ATT_WIDTH // bn
    qkv_blocks = 3 * per_section
    n_blocks = qkv_blocks + (per_section if with_gate else 0)
    gate_shift = (3 * N_GROUPS - 3) * per_section

    def col(j, i):
        return (0, jnp.where(j < qkv_blocks, g * qkv_blocks + j, j + gate_shift))

    return pl.pallas_call(
        functools.partial(_a_in_kernel, blocks_per_section=per_section),
        grid=(n_blocks, m // bm),
        in_specs=[
            pl.BlockSpec((bm, k), lambda j, i: (i, 0)),
            pl.BlockSpec((k, bn), col),
            pl.BlockSpec((1, bn), col),
        ],
        out_specs=pl.BlockSpec((bm, bn), lambda j, i: (i, j)),
        out_shape=jax.ShapeDtypeStruct((m, n_blocks * bn), jnp.bfloat16),
        scratch_shapes=[pltpu.VMEM((k, bn), jnp.bfloat16)],
        compiler_params=_params(2),
        name=f"a_in_proj_g{g}",
    )(h, w, colscale)


def _bias_kernel(table_ref, bucket_ref, o_ref):
    g = pl.program_id(0)
    hh = pl.program_id(1)
    bucket = bucket_ref[0]
    acc = jnp.full(bucket.shape, NEG, jnp.float32)
    for b in range(N_BUCKETS):
        acc = jnp.where(bucket == b, table_ref[b, g * N_HEADS + hh] * LOG2E, acc)
    col = lax.broadcasted_iota(jnp.int32, bucket.shape, 1)
    o_ref[0, 0, 0] = acc
    o_ref[0, 1, 0] = jnp.where(col < BLOCK, NEG, acc)


def _expand_bias(rel_bias, bucket):
    return pl.pallas_call(
        _bias_kernel,
        grid=(N_GROUPS, N_HEADS),
        in_specs=[
            pl.BlockSpec(memory_space=pltpu.SMEM),
            pl.BlockSpec((1, BLOCK, 2 * BLOCK), lambda g, h: (g, 0, 0)),
        ],
        out_specs=pl.BlockSpec((1, 2, 1, BLOCK, 2 * BLOCK), lambda g, h: (g, 0, h, 0, 0)),
        out_shape=jax.ShapeDtypeStruct((N_GROUPS, 2, N_HEADS, BLOCK, 2 * BLOCK), jnp.float32),
        compiler_params=_params(2),
        name="rel_bias_expand",
    )(rel_bias, bucket)


def _bucket_tiles():
    max_exact = N_BUCKETS // 2
    qi = jnp.arange(BLOCK)[:, None] + BLOCK
    kj = jnp.arange(2 * BLOCK)[None, :]
    step = qi - kj
    tiles = []
    for window, dilation in ATT_GROUPS:
        steps = window // dilation
        in_window = (step >= 0) & (step <= steps)
        dist = jnp.maximum(step, 0) * dilation
        is_small = dist < max_exact
        ratio = jnp.log(jnp.maximum(dist, 1).astype(jnp.float32) / max_exact) / math.log(MAX_DISTANCE / max_exact)
        large = max_exact + (ratio * (N_BUCKETS - max_exact)).astype(jnp.int32)
        large = jnp.minimum(large, N_BUCKETS - 1)
        bucket = jnp.where(is_small, dist, large)
        tiles.append(jnp.where(in_window, bucket, -1).astype(jnp.int32))
    return jnp.stack(tiles, axis=0)


def _attn_kernel(q_ref, k_ref, v_ref, bias_ref, o_ref, lse_ref, kk_ref, vv_ref):
    n = pl.program_id(1)

    @pl.when(n == 0)
    def _():
        kk_ref[0:BLOCK, :] = jnp.zeros((BLOCK, ATT_WIDTH), kk_ref.dtype)
        vv_ref[0:BLOCK, :] = jnp.zeros((BLOCK, ATT_WIDTH), vv_ref.dtype)

    kk_ref[BLOCK:2 * BLOCK, :] = k_ref[0]
    vv_ref[BLOCK:2 * BLOCK, :] = v_ref[0]

    variant = jnp.where(n == 0, 1, 0)
    lane = lax.broadcasted_iota(jnp.int32, (BLOCK, BLOCK), 1)
    lse_tile = jnp.zeros((BLOCK, BLOCK), jnp.float32)
    for hh in range(N_HEADS):
        sl = slice(hh * HEAD_DIM, (hh + 1) * HEAD_DIM)
        q = q_ref[0, :, sl]
        s = lax.dot_general(q, kk_ref[:, sl], (((1,), (1,)), ((), ())),
                            preferred_element_type=jnp.float32)
        s = s + bias_ref[0, variant, hh]
        m = jnp.max(s, axis=-1, keepdims=True)
        p = jnp.exp2(s - m)
        l = jnp.sum(p, axis=-1, keepdims=True)
        o = jnp.dot(p.astype(vv_ref.dtype), vv_ref[:, sl], preferred_element_type=jnp.float32)
        o_ref[0, :, sl] = (o / l).astype(o_ref.dtype)
        lse_tile = jnp.where(lane == hh, (m + jnp.log2(l)) * LN2, lse_tile)
    lse_ref[0] = lse_tile

    kk_ref[0:BLOCK, :] = k_ref[0]
    vv_ref[0:BLOCK, :] = v_ref[0]


def _attention(qkv, bias, g, batch, seq):
    _, d = ATT_GROUPS[g]
    sub_len = seq // d
    nb = sub_len // BLOCK
    qkv3 = qkv.reshape(batch * d, sub_len, qkv.shape[1])

    def qkv_spec(t):
        return pl.BlockSpec((1, BLOCK, ATT_WIDTH), lambda s, n: (s, n, t))

    return pl.pallas_call(
        _attn_kernel,
        grid=(batch * d, nb),
        in_specs=[
            qkv_spec(0), qkv_spec(1), qkv_spec(2),
            pl.BlockSpec((1, 2, N_HEADS, BLOCK, 2 * BLOCK), lambda s, n: (g, 0, 0, 0, 0)),
        ],
        out_specs=[
            pl.BlockSpec((1, BLOCK, ATT_WIDTH), lambda s, n: (s, n, 0)),
            pl.BlockSpec((1, BLOCK, BLOCK), lambda s, n: (s, n, 0)),
        ],
        out_shape=[
            jax.ShapeDtypeStruct((batch * d, sub_len, ATT_WIDTH), jnp.bfloat16),
            jax.ShapeDtypeStruct((batch * d, sub_len, BLOCK), jnp.float32),
        ],
        scratch_shapes=[
            pltpu.VMEM((2 * BLOCK, ATT_WIDTH), jnp.bfloat16),
            pltpu.VMEM((2 * BLOCK, ATT_WIDTH), jnp.bfloat16),
        ],
        compiler_params=_params(2),
        name=f"dilated_attn_g{g}",
    )(qkv3, qkv3, qkv3, bias)


def _a_out_kernel(o0_ref, o1_ref, o2_ref, l0_ref, l1_ref, l2_ref, z_ref, x_ref, w_ref,
                  out_ref, l1n_ref, l2n_ref, t1_ref, t2_ref, y_ref):
    bm = x_ref.shape[0]
    for r in range(4):
        l1n_ref[pl.ds(r, bm // 4, stride=4), :] = l1_ref[0, r]
    for r in range(16):
        l2n_ref[pl.ds(r, bm // 16, stride=16), :] = l2_ref[0, r]
    l0, l1, l2 = l0_ref[...], l1n_ref[...], l2n_ref[...]
    mx = jnp.maximum(jnp.maximum(l0, l1), l2)
    e0, e1, e2 = jnp.exp(l0 - mx), jnp.exp(l1 - mx), jnp.exp(l2 - mx)
    den = e0 + e1 + e2
    w0, w1, w2 = e0 / den, e1 / den, e2 / den
    acc = x_ref[...]
    for hh in range(N_HEADS):
        sl = slice(hh * HEAD_DIM, (hh + 1) * HEAD_DIM)
        for r in range(4):
            t1_ref[hh, pl.ds(r, bm // 4, stride=4), :] = o1_ref[0, r, :, sl].astype(jnp.float32)
        for r in range(16):
            t2_ref[hh, pl.ds(r, bm // 16, stride=16), :] = o2_ref[0, r, :, sl].astype(jnp.float32)
        o = (w0[:, hh:hh + 1] * o0_ref[:, sl].astype(jnp.float32)
             + w1[:, hh:hh + 1] * t1_ref[hh]
             + w2[:, hh:hh + 1] * t2_ref[hh])
        y_ref[:, sl] = (o * jax.nn.silu(z_ref[:, sl].astype(jnp.float32))).astype(y_ref.dtype)
        if hh % K_CHUNK_HEADS == K_CHUNK_HEADS - 1:
            ks = slice((hh + 1 - K_CHUNK_HEADS) * HEAD_DIM, (hh + 1) * HEAD_DIM)
            acc = acc + jnp.dot(y_ref[:, ks], w_ref[ks, :], preferred_element_type=jnp.float32)
    out_ref[...] = acc


def _a_out(os, lses, p0, x2, w_out, batch, seq, bm=256):
    m = x2.shape[0]
    per_seq = seq // bm
    z_block = 3
    row = lambda i: (i, 0)

    def perm_spec(d, width):
        return pl.BlockSpec((1, d, bm // d, width), lambda i: (i // per_seq, 0, i % per_seq, 0))

    def perm_view(a, d):
        return a.reshape(batch, d, seq // d, a.shape[-1])

    return pl.pallas_call(
        _a_out_kernel,
        grid=(m // bm,),
        in_specs=[
            pl.BlockSpec((bm, ATT_WIDTH), row), perm_spec(4, ATT_WIDTH), perm_spec(16, ATT_WIDTH),
            pl.BlockSpec((bm, BLOCK), row), perm_spec(4, BLOCK), perm_spec(16, BLOCK),
            pl.BlockSpec((bm, ATT_WIDTH), lambda i: (i, z_block)),
            pl.BlockSpec((bm, D_MODEL), row),
            pl.BlockSpec((ATT_WIDTH, D_MODEL), lambda i: (0, 0)),
        ],
        out_specs=pl.BlockSpec((bm, D_MODEL), row),
        out_shape=jax.ShapeDtypeStruct((m, D_MODEL), jnp.float32),
        scratch_shapes=[
            pltpu.VMEM((bm, BLOCK), jnp.float32),
            pltpu.VMEM((bm, BLOCK), jnp.float32),
            pltpu.VMEM((N_HEADS, bm, HEAD_DIM), jnp.float32),
            pltpu.VMEM((N_HEADS, bm, HEAD_DIM), jnp.float32),
            pltpu.VMEM((bm, ATT_WIDTH), jnp.bfloat16),
        ],
        compiler_params=_params(1),
        name="a_merge_out_proj",
    )(os[0].reshape(m, ATT_WIDTH), perm_view(os[1], 4), perm_view(os[2], 16),
      lses[0].reshape(m, BLOCK), perm_view(lses[1], 4), perm_view(lses[2], 16),
      p0, x2, w_out)


def _b_in_kernel(x_ref, g_ref, wa_ref, wg_ref, wz_ref, ba_ref, bg_ref, bz_ref,
                 u_ref, sz_ref, h_ref):
    @pl.when(pl.program_id(1) == 0)
    def _():
        _rmsnorm_to(h_ref, x_ref, g_ref, 128)

    h = h_ref[...]
    a = jnp.dot(h, wa_ref[...], preferred_element_type=jnp.float32) + ba_ref[...]
    ga = jnp.dot(h, wg_ref[...], preferred_element_type=jnp.float32) + bg_ref[...]
    u_ref[...] = a * jax.nn.sigmoid(ga)
    z = jnp.dot(h, wz_ref[...], preferred_element_type=jnp.float32) + bz_ref[...]
    sz_ref[...] = jax.nn.silu(z).astype(sz_ref.dtype)


def _b_in(x2, g, w, b, bm=1024, bn=512):
    m, k = x2.shape
    ch = w.shape[1] // 3
    nj = ch // bn

    def wspec(t):
        return pl.BlockSpec((k, bn), lambda i, j: (0, t * nj + j))

    def bspec(t):
        return pl.BlockSpec((1, bn), lambda i, j: (0, t * nj + j))

    return pl.pallas_call(
        _b_in_kernel,
        grid=(m // bm, nj),
        in_specs=[
            pl.BlockSpec((bm, k), lambda i, j: (i, 0)),
            pl.BlockSpec((1, k), lambda i, j: (0, 0)),
            wspec(0), wspec(1), wspec(2), bspec(0), bspec(1), bspec(2),
        ],
        out_specs=[
            pl.BlockSpec((bm, bn), lambda i, j: (i, j)),
            pl.BlockSpec((bm, bn), lambda i, j: (i, j)),
        ],
        out_shape=[
            jax.ShapeDtypeStruct((m, ch), jnp.float32),
            jax.ShapeDtypeStruct((m, ch), jnp.bfloat16),
        ],
        scratch_shapes=[pltpu.VMEM((bm, k), jnp.bfloat16)],
        compiler_params=_params(2),
        name="b_in_proj_glu",
    )(x2, g, w, w, w, b, b, b)


def _b_out_kernel(u_ref, halo_ref, sz_ref, x_ref, cw_ref, cb_ref, lg_ref, lb_ref,
                  w_ref, bo_ref, out_ref, ext_ref, conv_ref, y_ref, *, blocks_per_seq):
    bm = u_ref.shape[0]
    n_slabs = u_ref.shape[1] // 128
    i = pl.program_id(0)
    seq_start = (i % blocks_per_seq) == 0

    for c in range(n_slabs):
        sl = slice(c * 128, (c + 1) * 128)
        ext_ref[c, CONV_HALO:CONV_HALO + bm, :] = u_ref[:, sl]

    @pl.when(seq_start)
    def _():
        ext_ref[:, 0:CONV_HALO, :] = jnp.zeros((n_slabs, CONV_HALO, 128), jnp.float32)

    @pl.when(jnp.logical_not(seq_start))
    def _():
        for c in range(n_slabs):
            ext_ref[c, 0:CONV_HALO, :] = halo_ref[:, c * 128:(c + 1) * 128]

    first_tap = CONV_HALO - (CONV_WIDTH - 1)
    chunks = bm // CONV_ROWS

    def conv_body(t, carry):
        c = t // chunks
        r0 = pl.multiple_of((t % chunks) * CONV_ROWS, CONV_ROWS)
        acc = jnp.zeros((CONV_ROWS, 128), jnp.float32)
        for kk in range(CONV_WIDTH):
            acc = acc + cw_ref[c, kk:kk + 1, :] * ext_ref[c, pl.ds(r0 + first_tap + kk, CONV_ROWS), :]
        conv_ref[c, pl.ds(r0, CONV_ROWS), :] = acc
        return carry

    lax.fori_loop(0, n_slabs * chunks, conv_body, 0, unroll=4)

    tot = jnp.zeros((bm, 128), jnp.float32)
    for c in range(n_slabs):
        sl = slice(c * 128, (c + 1) * 128)
        tot = tot + (conv_ref[c] + cb_ref[:, sl])
    mu = jnp.sum(tot, axis=-1, keepdims=True) / (n_slabs * 128)
    sq = jnp.zeros((bm, 128), jnp.float32)
    for c in range(n_slabs):
        sl = slice(c * 128, (c + 1) * 128)
        dv = conv_ref[c] + cb_ref[:, sl] - mu
        sq = sq + dv * dv
    var = jnp.sum(sq, axis=-1, keepdims=True) / (n_slabs * 128)
    inv = lax.rsqrt(var + EPS)
    acc = x_ref[...] + bo_ref[...]
    for c in range(n_slabs):
        sl = slice(c * 128, (c + 1) * 128)
        uf = (conv_ref[c] + cb_ref[:, sl] - mu) * inv * lg_ref[:, sl] + lb_ref[:, sl]
        y_ref[:, sl] = (jax.nn.silu(uf) * sz_ref[:, sl].astype(jnp.float32)).astype(y_ref.dtype)
        if c % K_CHUNK_HEADS == K_CHUNK_HEADS - 1:
            ks = slice((c + 1 - K_CHUNK_HEADS) * 128, (c + 1) * 128)
            acc = acc + jnp.dot(y_ref[:, ks], w_ref[ks, :], preferred_element_type=jnp.float32)
    out_ref[...] = acc


def _b_out(u, sz, x2, conv_w_slabs, conv_b, ln_g, ln_b, w_out, b_out, seq, bm=256):
    m, ch = u.shape
    n_slabs = ch // 128
    halo_blocks = bm // CONV_HALO
    row = lambda i: (i, 0)
    const2 = lambda i: (0, 0)
    return pl.pallas_call(
        functools.partial(_b_out_kernel, blocks_per_seq=seq // bm),
        grid=(m // bm,),
        in_specs=[
            pl.BlockSpec((bm, ch), row),
            pl.BlockSpec((CONV_HALO, ch), lambda i: (jnp.maximum(i * halo_blocks - 1, 0), 0)),
            pl.BlockSpec((bm, ch), row),
            pl.BlockSpec((bm, D_MODEL), row),
            pl.BlockSpec((n_slabs, CONV_HALO, 128), lambda i: (0, 0, 0)),
            pl.BlockSpec((1, ch), const2), pl.BlockSpec((1, ch), const2),
            pl.BlockSpec((1, ch), const2),
            pl.BlockSpec((ch, D_MODEL), const2),
            pl.BlockSpec((1, D_MODEL), const2),
        ],
        out_specs=pl.BlockSpec((bm, D_MODEL), row),
        out_shape=jax.ShapeDtypeStruct((m, D_MODEL), jnp.float32),
        scratch_shapes=[
            pltpu.VMEM((n_slabs, CONV_HALO + bm, 128), jnp.float32),
            pltpu.VMEM((n_slabs, bm, 128), jnp.float32),
            pltpu.VMEM((bm, ch), jnp.bfloat16),
        ],
        compiler_params=_params(1),
        name="b_conv_ln_out_proj",
    )(u, u, sz, x2, conv_w_slabs, conv_b, ln_g, ln_b, w_out, b_out)


def kernel(x, norm_g, rel_bias, a_w_in, a_q_gain, a_k_gain, a_w_out, b_w_in, b_b_in,
           b_conv_w, b_conv_b, b_ln_g, b_ln_b, b_w_out, b_b_out):
    batch, seq, dm = x.shape
    m = batch * seq
    bf16 = jnp.bfloat16
    x2 = x.reshape(m, dm)

    ones = jnp.ones((ATT_WIDTH,), jnp.float32)
    cols = []
    for g in range(N_GROUPS):
        cols += [jnp.tile(a_q_gain[0, g], N_HEADS) * (HEAD_DIM ** -0.5 * LOG2E),
                 jnp.tile(a_k_gain[0, g], N_HEADS), ones]
    cols.append(ones)
    colscale = jnp.concatenate(cols).reshape(1, A_COLS)

    w_in = a_w_in[0]
    hs = _a_norm(x2, norm_g[0:1], batch, seq)
    bias = _expand_bias(rel_bias, _bucket_tiles())
    os, lses, p0 = [], [], None
    for g in range(N_GROUPS):
        pg = _a_in(hs[g].reshape(m, dm), w_in, colscale, g, with_gate=(g == 0))
        if g == 0:
            p0 = pg
        o, lse = _attention(pg, bias, g, batch, seq)
        os.append(o)
        lses.append(lse)
    x2 = _a_out(os, lses, p0, x2, a_w_out[0].astype(bf16), batch, seq)

    u, sz = _b_in(x2, norm_g[1:2], b_w_in[0].astype(bf16), b_b_in[0:1])
    ch = u.shape[1]
    cw = jnp.pad(b_conv_w[0], ((0, CONV_HALO - CONV_WIDTH), (0, 0)))
    cw = cw.reshape(CONV_HALO, ch // 128, 128).transpose(1, 0, 2)
    x2 = _b_out(u, sz, x2, cw, b_conv_b[0:1], b_ln_g[0:1], b_ln_b[0:1],
                b_w_out[0].astype(bf16), b_b_out[0:1], seq)
    return x2.reshape(batch, seq, dm)
```

```python
import functools
import math

import jax
import jax.numpy as jnp
from jax import lax
from jax.experimental import pallas as pl
from jax.experimental.pallas import tpu as pltpu

D_MODEL = 2048
HEAD_DIM = 128
N_HEADS = 16
ATT_GROUPS = ((128, 1), (512, 4), (2048, 16))
N_GROUPS = 3
ATT_WIDTH = N_HEADS * HEAD_DIM
A_COLS = 3 * N_GROUPS * ATT_WIDTH + ATT_WIDTH
BLOCK = 128
N_BUCKETS = 32
MAX_DISTANCE = 2048
CONV_WIDTH = 31
EPS = 1e-6
NEG = -1e30
LOG2E = math.log2(math.e)
LN2 = math.log(2.0)

VMEM_LIMIT_BYTES = 56 * 1024 * 1024
CONV_HALO = 32
CONV_ROWS = 64
ATT_QB = 2
K_CHUNK_HEADS = 2


def _params(n_axes):
    return pltpu.CompilerParams(
        dimension_semantics=("arbitrary",) * n_axes,
        vmem_limit_bytes=VMEM_LIMIT_BYTES)


def _rmsnorm_to(h_ref, x_ref, g_ref, rows):
    n_chunks = x_ref.shape[0] // rows

    def body(c, carry):
        r0 = pl.multiple_of(c * rows, rows)
        xc = x_ref[pl.ds(r0, rows), :]
        ms = jnp.mean(xc * xc, axis=-1, keepdims=True)
        h_ref[pl.ds(r0, rows), :] = (xc * lax.rsqrt(ms + EPS) * g_ref[...]).astype(h_ref.dtype)
        return carry

    lax.fori_loop(0, n_chunks, body, 0)


def _a_norm_kernel(x_ref, g_ref, hn_ref, h4_ref, h16_ref, slab_ref):
    bm = x_ref.shape[0]
    n_slabs = x_ref.shape[1] // 128
    rows = 128
    for c0 in range(bm // rows):
        rs = slice(c0 * rows, (c0 + 1) * rows)
        xc = x_ref[rs, :]
        ms = jnp.mean(xc * xc, axis=-1, keepdims=True)
        hn = xc * lax.rsqrt(ms + EPS) * g_ref[...]
        hn_ref[rs, :] = hn.astype(hn_ref.dtype)
        for c in range(n_slabs):
            slab_ref[c, rs, :] = hn[:, c * 128:(c + 1) * 128]
    for c in range(n_slabs):
        sl = slice(c * 128, (c + 1) * 128)
        for r in range(4):
            h4_ref[0, r, :, sl] = slab_ref[c, pl.ds(r, bm // 4, stride=4), :].astype(h4_ref.dtype)
        for r in range(16):
            h16_ref[0, r, :, sl] = slab_ref[c, pl.ds(r, bm // 16, stride=16), :].astype(h16_ref.dtype)


def _a_norm(x2, g, batch, seq, bm=512):
    m, k = x2.shape
    per_seq = seq // bm
    bf16 = jnp.bfloat16

    def perm_spec(d):
        return pl.BlockSpec((1, d, bm // d, k), lambda i: (i // per_seq, 0, i % per_seq, 0))

    return pl.pallas_call(
        _a_norm_kernel,
        grid=(m // bm,),
        in_specs=[pl.BlockSpec((bm, k), lambda i: (i, 0)), pl.BlockSpec((1, k), lambda i: (0, 0))],
        out_specs=[pl.BlockSpec((bm, k), lambda i: (i, 0)), perm_spec(4), perm_spec(16)],
        out_shape=[
            jax.ShapeDtypeStruct((m, k), bf16),
            jax.ShapeDtypeStruct((batch, 4, seq // 4, k), bf16),
            jax.ShapeDtypeStruct((batch, 16, seq // 16, k), bf16),
        ],
        scratch_shapes=[pltpu.VMEM((k // 128, bm, 128), jnp.float32)],
        compiler_params=_params(1),
        name="a_rmsnorm_permute",
    )(x2, g)


def _a_in_kernel(h_ref, w_ref, cs_ref, o_ref, wb_ref, *, blocks_per_section):
    @pl.when(pl.program_id(1) == 0)
    def _():
        rows = 256
        for c0 in range(w_ref.shape[0] // rows):
            rs = slice(c0 * rows, (c0 + 1) * rows)
            wb_ref[rs, :] = w_ref[rs, :].astype(wb_ref.dtype)

    acc = jnp.dot(h_ref[...], wb_ref[...], preferred_element_type=jnp.float32)
    is_q = pl.program_id(0) < blocks_per_section
    for hh in range(acc.shape[1] // HEAD_DIM):
        sl = slice(hh * HEAD_DIM, (hh + 1) * HEAD_DIM)
        a = acc[:, sl]
        ms = jnp.mean(a * a, axis=-1, keepdims=True)
        scale = jnp.where(is_q, lax.rsqrt(ms + EPS) * cs_ref[:, sl], 1.0)
        o_ref[:, sl] = (a * scale).astype(o_ref.dtype)


def _a_in(h, w, colscale, g, with_gate, bm=1024, bn=1024):
    m, k = h.shape
    per_section = ATT_WIDTH // bn
    n_blocks = (3 if with_gate else 2) * per_section
    group0 = 3 * g * per_section
    gate0 = 3 * N_GROUPS * per_section

    def col(j, i):
        sec, off = j // per_section, j % per_section
        return (0, jnp.where(sec == 2, gate0 + off, group0 + 2 * sec * per_section + off))

    return pl.pallas_call(
        functools.partial(_a_in_kernel, blocks_per_section=per_section),
        grid=(n_blocks, m // bm),
        in_specs=[
            pl.BlockSpec((bm, k), lambda j, i: (i, 0)),
            pl.BlockSpec((k, bn), col),
            pl.BlockSpec((1, bn), col),
        ],
        out_specs=pl.BlockSpec((bm, bn), lambda j, i: (i, j)),
        out_shape=jax.ShapeDtypeStruct((m, n_blocks * bn), jnp.bfloat16),
        scratch_shapes=[pltpu.VMEM((k, bn), jnp.bfloat16)],
        compiler_params=_params(2),
        name=f"a_in_proj_qv_g{g}",
    )(h, w, colscale)


def _a_in_k_kernel(h_ref, w_ref, gain_ref, o_ref, wt_ref):
    @pl.when(pl.program_id(1) == 0)
    def _():
        rows = 256
        for c0 in range(w_ref.shape[0] // rows):
            rs = slice(c0 * rows, (c0 + 1) * rows)
            wt_ref[:, rs] = w_ref[rs, :].T.astype(wt_ref.dtype)

    acc = lax.dot_general(wt_ref[...], h_ref[...], (((1,), (1,)), ((), ())),
                          preferred_element_type=jnp.float32)
    bm = acc.shape[1]
    for hh in range(acc.shape[0] // HEAD_DIM):
        hs = slice(hh * HEAD_DIM, (hh + 1) * HEAD_DIM)
        a = acc[hs, :]
        ms = jnp.mean(a * a, axis=0, keepdims=True)
        an = a * lax.rsqrt(ms + EPS)
        for lb in range(bm // 128):
            ls = slice(lb * 128, (lb + 1) * 128)
            o_ref[hs, ls] = (an[:, ls] * gain_ref[0]).astype(o_ref.dtype)


def _a_in_k(h, w, gain_tile, g, bm=1024, bn=1024):
    m, k = h.shape
    per_section = ATT_WIDTH // bn
    k0 = (3 * g + 1) * per_section
    return pl.pallas_call(
        _a_in_k_kernel,
        grid=(per_section, m // bm),
        in_specs=[
            pl.BlockSpec((bm, k), lambda j, i: (i, 0)),
            pl.BlockSpec((k, bn), lambda j, i: (0, k0 + j)),
            pl.BlockSpec((1, HEAD_DIM, 128), lambda j, i: (g, 0, 0)),
        ],
        out_specs=pl.BlockSpec((bn, bm), lambda j, i: (j, i)),
        out_shape=jax.ShapeDtypeStruct((ATT_WIDTH, m), jnp.bfloat16),
        scratch_shapes=[pltpu.VMEM((bn, k), jnp.bfloat16)],
        compiler_params=_params(2),
        name=f"a_in_proj_kT_g{g}",
    )(h, w, gain_tile)


def _bias_kernel(table_ref, bucket_ref, o_ref):
    g = pl.program_id(0)
    hh = pl.program_id(1)
    bucket = bucket_ref[0]
    acc = jnp.full(bucket.shape, NEG, jnp.float32)
    for b in range(N_BUCKETS):
        acc = jnp.where(bucket == b, table_ref[b, g * N_HEADS + hh] * LOG2E, acc)
    col = lax.broadcasted_iota(jnp.int32, bucket.shape, 1)
    o_ref[0, 0, 0] = acc
    o_ref[0, 1, 0] = jnp.where(col < BLOCK, NEG, acc)


def _expand_bias(rel_bias, bucket):
    return pl.pallas_call(
        _bias_kernel,
        grid=(N_GROUPS, N_HEADS),
        in_specs=[
            pl.BlockSpec(memory_space=pltpu.SMEM),
            pl.BlockSpec((1, BLOCK, 2 * BLOCK), lambda g, h: (g, 0, 0)),
        ],
        out_specs=pl.BlockSpec((1, 2, 1, BLOCK, 2 * BLOCK), lambda g, h: (g, 0, h, 0, 0)),
        out_shape=jax.ShapeDtypeStruct((N_GROUPS, 2, N_HEADS, BLOCK, 2 * BLOCK), jnp.float32),
        compiler_params=_params(2),
        name="rel_bias_expand",
    )(rel_bias, bucket)


def _bucket_tiles():
    max_exact = N_BUCKETS // 2
    qi = jnp.arange(BLOCK)[:, None] + BLOCK
    kj = jnp.arange(2 * BLOCK)[None, :]
    step = qi - kj
    tiles = []
    for window, dilation in ATT_GROUPS:
        steps = window // dilation
        in_window = (step >= 0) & (step <= steps)
        dist = jnp.maximum(step, 0) * dilation
        is_small = dist < max_exact
        ratio = jnp.log(jnp.maximum(dist, 1).astype(jnp.float32) / max_exact) / math.log(MAX_DISTANCE / max_exact)
        large = max_exact + (ratio * (N_BUCKETS - max_exact)).astype(jnp.int32)
        large = jnp.minimum(large, N_BUCKETS - 1)
        bucket = jnp.where(is_small, dist, large)
        tiles.append(jnp.where(in_window, bucket, -1).astype(jnp.int32))
    return jnp.stack(tiles, axis=0)


def _attn_kernel(q_ref, kt_ref, ktp_ref, v_ref, vp_ref, bias_ref, o_ref, lse_ref, kk_ref, vv_ref):
    n = pl.program_id(1)
    span = ATT_QB * BLOCK
    kk_ref[:, 0:BLOCK] = ktp_ref[...]
    kk_ref[:, BLOCK:BLOCK + span] = kt_ref[...]
    vv_ref[0:BLOCK, :] = vp_ref[0]
    vv_ref[BLOCK:BLOCK + span, :] = v_ref[0]

    lane = lax.broadcasted_iota(jnp.int32, (BLOCK, BLOCK), 1)
    for sb in range(ATT_QB):
        rows = slice(sb * BLOCK, (sb + 1) * BLOCK)
        keys = slice(sb * BLOCK, (sb + 2) * BLOCK)
        variant = jnp.where(n == 0, 1, 0) if sb == 0 else 0
        lse_tile = jnp.zeros((BLOCK, BLOCK), jnp.float32)
        for hh in range(N_HEADS):
            sl = slice(hh * HEAD_DIM, (hh + 1) * HEAD_DIM)
            s = jnp.dot(q_ref[0, rows, sl], kk_ref[sl, keys], preferred_element_type=jnp.float32)
            s = s + bias_ref[0, variant, hh]
            m = jnp.max(s, axis=-1, keepdims=True)
            p = jnp.exp2(s - m)
            l = jnp.sum(p, axis=-1, keepdims=True)
            o = jnp.dot(p.astype(vv_ref.dtype), vv_ref[keys, sl], preferred_element_type=jnp.float32)
            o_ref[0, rows, sl] = (o / l).astype(o_ref.dtype)
            lse_tile = jnp.where(lane == hh, (m + jnp.log2(l)) * LN2, lse_tile)
        lse_ref[0, rows, :] = lse_tile


def _attention(qv, kt, bias, g, batch, seq):
    _, d = ATT_GROUPS[g]
    sub_len = seq // d
    span = ATT_QB * BLOCK
    nb = sub_len // span
    qv3 = qv.reshape(batch * d, sub_len, qv.shape[1])

    return pl.pallas_call(
        _attn_kernel,
        grid=(batch * d, nb),
        in_specs=[
            pl.BlockSpec((1, span, ATT_WIDTH), lambda s, n: (s, n, 0)),
            pl.BlockSpec((ATT_WIDTH, span), lambda s, n: (0, s * nb + n)),
            pl.BlockSpec((ATT_WIDTH, BLOCK),
                         lambda s, n: (0, jnp.maximum((s * nb + n) * ATT_QB - 1, 0))),
            pl.BlockSpec((1, span, ATT_WIDTH), lambda s, n: (s, n, 1)),
            pl.BlockSpec((1, BLOCK, ATT_WIDTH), lambda s, n: (s, jnp.maximum(n * ATT_QB - 1, 0), 1)),
            pl.BlockSpec((1, 2, N_HEADS, BLOCK, 2 * BLOCK), lambda s, n: (g, 0, 0, 0, 0)),
        ],
        out_specs=[
            pl.BlockSpec((1, span, ATT_WIDTH), lambda s, n: (s, n, 0)),
            pl.BlockSpec((1, span, BLOCK), lambda s, n: (s, n, 0)),
        ],
        out_shape=[
            jax.ShapeDtypeStruct((batch * d, sub_len, ATT_WIDTH), jnp.bfloat16),
            jax.ShapeDtypeStruct((batch * d, sub_len, BLOCK), jnp.float32),
        ],
        scratch_shapes=[
            pltpu.VMEM((ATT_WIDTH, BLOCK + span), jnp.bfloat16),
            pltpu.VMEM((BLOCK + span, ATT_WIDTH), jnp.bfloat16),
        ],
        compiler_params=_params(2),
        name=f"dilated_attn_g{g}",
    )(qv3, kt, kt, qv3, qv3, bias)


def _a_out_kernel(o0_ref, o1_ref, o2_ref, l0_ref, l1_ref, l2_ref, z_ref, x_ref, w_ref,
                  out_ref, l1n_ref, l2n_ref, t1_ref, t2_ref, y_ref):
    bm = x_ref.shape[0]
    for r in range(4):
        l1n_ref[pl.ds(r, bm // 4, stride=4), :] = l1_ref[0, r]
    for r in range(16):
        l2n_ref[pl.ds(r, bm // 16, stride=16), :] = l2_ref[0, r]
    l0, l1, l2 = l0_ref[...], l1n_ref[...], l2n_ref[...]
    mx = jnp.maximum(jnp.maximum(l0, l1), l2)
    e0, e1, e2 = jnp.exp(l0 - mx), jnp.exp(l1 - mx), jnp.exp(l2 - mx)
    den = e0 + e1 + e2
    w0, w1, w2 = e0 / den, e1 / den, e2 / den
    acc = x_ref[...]
    for hh in range(N_HEADS):
        sl = slice(hh * HEAD_DIM, (hh + 1) * HEAD_DIM)
        for r in range(4):
            t1_ref[hh, pl.ds(r, bm // 4, stride=4), :] = o1_ref[0, r, :, sl].astype(jnp.float32)
        for r in range(16):
            t2_ref[hh, pl.ds(r, bm // 16, stride=16), :] = o2_ref[0, r, :, sl].astype(jnp.float32)
        o = (w0[:, hh:hh + 1] * o0_ref[:, sl].astype(jnp.float32)
             + w1[:, hh:hh + 1] * t1_ref[hh]
             + w2[:, hh:hh + 1] * t2_ref[hh])
        y_ref[:, sl] = (o * jax.nn.silu(z_ref[:, sl].astype(jnp.float32))).astype(y_ref.dtype)
        if hh % K_CHUNK_HEADS == K_CHUNK_HEADS - 1:
            ks = slice((hh + 1 - K_CHUNK_HEADS) * HEAD_DIM, (hh + 1) * HEAD_DIM)
            acc = acc + jnp.dot(y_ref[:, ks], w_ref[ks, :], preferred_element_type=jnp.float32)
    out_ref[...] = acc


def _a_out(os, lses, p0, x2, w_out, batch, seq, bm=256):
    m = x2.shape[0]
    per_seq = seq // bm
    z_block = 2
    row = lambda i: (i, 0)

    def perm_spec(d, width):
        return pl.BlockSpec((1, d, bm // d, width), lambda i: (i // per_seq, 0, i % per_seq, 0))

    def perm_view(a, d):
        return a.reshape(batch, d, seq // d, a.shape[-1])

    return pl.pallas_call(
        _a_out_kernel,
        grid=(m // bm,),
        in_specs=[
            pl.BlockSpec((bm, ATT_WIDTH), row), perm_spec(4, ATT_WIDTH), perm_spec(16, ATT_WIDTH),
            pl.BlockSpec((bm, BLOCK), row), perm_spec(4, BLOCK), perm_spec(16, BLOCK),
            pl.BlockSpec((bm, ATT_WIDTH), lambda i: (i, z_block)),
            pl.BlockSpec((bm, D_MODEL), row),
            pl.BlockSpec((ATT_WIDTH, D_MODEL), lambda i: (0, 0)),
        ],
        out_specs=pl.BlockSpec((bm, D_MODEL), row),
        out_shape=jax.ShapeDtypeStruct((m, D_MODEL), jnp.float32),
        scratch_shapes=[
            pltpu.VMEM((bm, BLOCK), jnp.float32),
            pltpu.VMEM((bm, BLOCK), jnp.float32),
            pltpu.VMEM((N_HEADS, bm, HEAD_DIM), jnp.float32),
            pltpu.VMEM((N_HEADS, bm, HEAD_DIM), jnp.float32),
            pltpu.VMEM((bm, ATT_WIDTH), jnp.bfloat16),
        ],
        compiler_params=_params(1),
        name="a_merge_out_proj",
    )(os[0].reshape(m, ATT_WIDTH), perm_view(os[1], 4), perm_view(os[2], 16),
      lses[0].reshape(m, BLOCK), perm_view(lses[1], 4), perm_view(lses[2], 16),
      p0, x2, w_out)


def _b_in_kernel(x_ref, g_ref, wa_ref, wg_ref, wz_ref, ba_ref, bg_ref, bz_ref,
                 u_ref, sz_ref, h_ref):
    @pl.when(pl.program_id(1) == 0)
    def _():
        _rmsnorm_to(h_ref, x_ref, g_ref, 128)

    h = h_ref[...]
    a = jnp.dot(h, wa_ref[...], preferred_element_type=jnp.float32) + ba_ref[...]
    ga = jnp.dot(h, wg_ref[...], preferred_element_type=jnp.float32) + bg_ref[...]
    u_ref[...] = a * jax.nn.sigmoid(ga)
    z = jnp.dot(h, wz_ref[...], preferred_element_type=jnp.float32) + bz_ref[...]
    sz_ref[...] = jax.nn.silu(z).astype(sz_ref.dtype)


def _b_in(x2, g, w, b, bm=1024, bn=512):
    m, k = x2.shape
    ch = w.shape[1] // 3
    nj = ch // bn

    def wspec(t):
        return pl.BlockSpec((k, bn), lambda i, j: (0, t * nj + j))

    def bspec(t):
        return pl.BlockSpec((1, bn), lambda i, j: (0, t * nj + j))

    return pl.pallas_call(
        _b_in_kernel,
        grid=(m // bm, nj),
        in_specs=[
            pl.BlockSpec((bm, k), lambda i, j: (i, 0)),
            pl.BlockSpec((1, k), lambda i, j: (0, 0)),
            wspec(0), wspec(1), wspec(2), bspec(0), bspec(1), bspec(2),
        ],
        out_specs=[
            pl.BlockSpec((bm, bn), lambda i, j: (i, j)),
            pl.BlockSpec((bm, bn), lambda i, j: (i, j)),
        ],
        out_shape=[
            jax.ShapeDtypeStruct((m, ch), jnp.float32),
            jax.ShapeDtypeStruct((m, ch), jnp.bfloat16),
        ],
        scratch_shapes=[pltpu.VMEM((bm, k), jnp.bfloat16)],
        compiler_params=_params(2),
        name="b_in_proj_glu",
    )(x2, g, w, w, w, b, b, b)


def _b_out_kernel(u_ref, halo_ref, sz_ref, x_ref, cw_ref, cb_ref, lg_ref, lb_ref,
                  w_ref, bo_ref, out_ref, ext_ref, conv_ref, y_ref, *, blocks_per_seq):
    bm = u_ref.shape[0]
    n_slabs = u_ref.shape[1] // 128
    i = pl.program_id(0)
    seq_start = (i % blocks_per_seq) == 0

    for c in range(n_slabs):
        sl = slice(c * 128, (c + 1) * 128)
        ext_ref[c, CONV_HALO:CONV_HALO + bm, :] = u_ref[:, sl]

    @pl.when(seq_start)
    def _():
        ext_ref[:, 0:CONV_HALO, :] = jnp.zeros((n_slabs, CONV_HALO, 128), jnp.float32)

    @pl.when(jnp.logical_not(seq_start))
    def _():
        for c in range(n_slabs):
            ext_ref[c, 0:CONV_HALO, :] = halo_ref[:, c * 128:(c + 1) * 128]

    first_tap = CONV_HALO - (CONV_WIDTH - 1)
    chunks = bm // CONV_ROWS

    def conv_body(t, carry):
        c = t // chunks
        r0 = pl.multiple_of((t % chunks) * CONV_ROWS, CONV_ROWS)
        acc = jnp.zeros((CONV_ROWS, 128), jnp.float32)
        for kk in range(CONV_WIDTH):
            acc = acc + cw_ref[c, kk:kk + 1, :] * ext_ref[c, pl.ds(r0 + first_tap + kk, CONV_ROWS), :]
        conv_ref[c, pl.ds(r0, CONV_ROWS), :] = acc
        return carry

    lax.fori_loop(0, n_slabs * chunks, conv_body, 0, unroll=4)

    tot = jnp.zeros((bm, 128), jnp.float32)
    for c in range(n_slabs):
        sl = slice(c * 128, (c + 1) * 128)
        tot = tot + (conv_ref[c] + cb_ref[:, sl])
    mu = jnp.sum(tot, axis=-1, keepdims=True) / (n_slabs * 128)
    sq = jnp.zeros((bm, 128), jnp.float32)
    for c in range(n_slabs):
        sl = slice(c * 128, (c + 1) * 128)
        dv = conv_ref[c] + cb_ref[:, sl] - mu
        sq = sq + dv * dv
    var = jnp.sum(sq, axis=-1, keepdims=True) / (n_slabs * 128)
    inv = lax.rsqrt(var + EPS)
    acc = x_ref[...] + bo_ref[...]
    for c in range(n_slabs):
        sl = slice(c * 128, (c + 1) * 128)
        uf = (conv_ref[c] + cb_ref[:, sl] - mu) * inv * lg_ref[:, sl] + lb_ref[:, sl]
        y_ref[:, sl] = (jax.nn.silu(uf) * sz_ref[:, sl].astype(jnp.float32)).astype(y_ref.dtype)
        if c % K_CHUNK_HEADS == K_CHUNK_HEADS - 1:
            ks = slice((c + 1 - K_CHUNK_HEADS) * 128, (c + 1) * 128)
            acc = acc + jnp.dot(y_ref[:, ks], w_ref[ks, :], preferred_element_type=jnp.float32)
    out_ref[...] = acc


def _b_out(u, sz, x2, conv_w_slabs, conv_b, ln_g, ln_b, w_out, b_out, seq, bm=256):
    m, ch = u.shape
    n_slabs = ch // 128
    halo_blocks = bm // CONV_HALO
    row = lambda i: (i, 0)
    const2 = lambda i: (0, 0)
    return pl.pallas_call(
        functools.partial(_b_out_kernel, blocks_per_seq=seq // bm),
        grid=(m // bm,),
        in_specs=[
            pl.BlockSpec((bm, ch), row),
            pl.BlockSpec((CONV_HALO, ch), lambda i: (jnp.maximum(i * halo_blocks - 1, 0), 0)),
            pl.BlockSpec((bm, ch), row),
            pl.BlockSpec((bm, D_MODEL), row),
            pl.BlockSpec((n_slabs, CONV_HALO, 128), lambda i: (0, 0, 0)),
            pl.BlockSpec((1, ch), const2), pl.BlockSpec((1, ch), const2),
            pl.BlockSpec((1, ch), const2),
            pl.BlockSpec((ch, D_MODEL), const2),
            pl.BlockSpec((1, D_MODEL), const2),
        ],
        out_specs=pl.BlockSpec((bm, D_MODEL), row),
        out_shape=jax.ShapeDtypeStruct((m, D_MODEL), jnp.float32),
        scratch_shapes=[
            pltpu.VMEM((n_slabs, CONV_HALO + bm, 128), jnp.float32),
            pltpu.VMEM((n_slabs, bm, 128), jnp.float32),
            pltpu.VMEM((bm, ch), jnp.bfloat16),
        ],
        compiler_params=_params(1),
        name="b_conv_ln_out_proj",
    )(u, u, sz, x2, conv_w_slabs, conv_b, ln_g, ln_b, w_out, b_out)


def kernel(x, norm_g, rel_bias, a_w_in, a_q_gain, a_k_gain, a_w_out, b_w_in, b_b_in,
           b_conv_w, b_conv_b, b_ln_g, b_ln_b, b_w_out, b_b_out):
    batch, seq, dm = x.shape
    m = batch * seq
    bf16 = jnp.bfloat16
    x2 = x.reshape(m, dm)

    ones = jnp.ones((ATT_WIDTH,), jnp.float32)
    cols = []
    for g in range(N_GROUPS):
        cols += [jnp.tile(a_q_gain[0, g], N_HEADS) * (HEAD_DIM ** -0.5 * LOG2E), ones, ones]
    cols.append(ones)
    colscale = jnp.concatenate(cols).reshape(1, A_COLS)
    k_gain_tile = jnp.broadcast_to(a_k_gain[0][:, :, None], (N_GROUPS, HEAD_DIM, 128))

    w_in = a_w_in[0]
    hs = _a_norm(x2, norm_g[0:1], batch, seq)
    bias = _expand_bias(rel_bias, _bucket_tiles())
    os, lses, p0 = [], [], None
    for g in range(N_GROUPS):
        hg = hs[g].reshape(m, dm)
        qv = _a_in(hg, w_in, colscale, g, with_gate=(g == 0))
        kt = _a_in_k(hg, w_in, k_gain_tile, g)
        if g == 0:
            p0 = qv
        o, lse = _attention(qv, kt, bias, g, batch, seq)
        os.append(o)
        lses.append(lse)
    x2 = _a_out(os, lses, p0, x2, a_w_out[0].astype(bf16), batch, seq)

    u, sz = _b_in(x2, norm_g[1:2], b_w_in[0].astype(bf16), b_b_in[0:1])
    ch = u.shape[1]
    cw = jnp.pad(b_conv_w[0], ((0, CONV_HALO - CONV_WIDTH), (0, 0)))
    cw = cw.reshape(CONV_HALO, ch // 128, 128).transpose(1, 0, 2)
    x2 = _b_out(u, sz, x2, cw, b_conv_b[0:1], b_ln_g[0:1], b_ln_b[0:1],
                b_w_out[0].astype(bf16), b_b_out[0:1], seq)
    return x2.reshape(batch, seq, dm)
```

```python
import functools
import math

import jax
import jax.numpy as jnp
from jax import lax
from jax.experimental import pallas as pl
from jax.experimental.pallas import tpu as pltpu

D_MODEL = 2048
HEAD_DIM = 128
N_HEADS = 16
ATT_GROUPS = ((128, 1), (512, 4), (2048, 16))
N_GROUPS = 3
ATT_WIDTH = N_HEADS * HEAD_DIM
A_COLS = 3 * N_GROUPS * ATT_WIDTH + ATT_WIDTH
BLOCK = 128
N_BUCKETS = 32
MAX_DISTANCE = 2048
CONV_WIDTH = 31
EPS = 1e-6
NEG = -1e30
LOG2E = math.log2(math.e)
LN2 = math.log(2.0)

VMEM_LIMIT_BYTES = 56 * 1024 * 1024
CONV_HALO = 32
CONV_ROWS = 64
ATT_QB = 4
K_CHUNK_HEADS = 2


def _params(n_axes):
    return pltpu.CompilerParams(
        dimension_semantics=("arbitrary",) * n_axes,
        vmem_limit_bytes=VMEM_LIMIT_BYTES)


def _rmsnorm_to(h_ref, x_ref, g_ref, rows):
    n_chunks = x_ref.shape[0] // rows

    def body(c, carry):
        r0 = pl.multiple_of(c * rows, rows)
        xc = x_ref[pl.ds(r0, rows), :]
        ms = jnp.mean(xc * xc, axis=-1, keepdims=True)
        h_ref[pl.ds(r0, rows), :] = (xc * lax.rsqrt(ms + EPS) * g_ref[...]).astype(h_ref.dtype)
        return carry

    lax.fori_loop(0, n_chunks, body, 0)


def _a_norm_kernel(x_ref, g_ref, hn_ref, h4_ref, h16_ref, slab_ref):
    bm = x_ref.shape[0]
    n_slabs = x_ref.shape[1] // 128
    rows = 128
    for c0 in range(bm // rows):
        rs = slice(c0 * rows, (c0 + 1) * rows)
        xc = x_ref[rs, :]
        ms = jnp.mean(xc * xc, axis=-1, keepdims=True)
        hn = xc * lax.rsqrt(ms + EPS) * g_ref[...]
        hn_ref[rs, :] = hn.astype(hn_ref.dtype)
        for c in range(n_slabs):
            slab_ref[c, rs, :] = hn[:, c * 128:(c + 1) * 128]
    for c in range(n_slabs):
        sl = slice(c * 128, (c + 1) * 128)
        for r in range(4):
            h4_ref[0, r, :, sl] = slab_ref[c, pl.ds(r, bm // 4, stride=4), :].astype(h4_ref.dtype)
        for r in range(16):
            h16_ref[0, r, :, sl] = slab_ref[c, pl.ds(r, bm // 16, stride=16), :].astype(h16_ref.dtype)


def _a_norm(x2, g, batch, seq, bm=512):
    m, k = x2.shape
    per_seq = seq // bm
    bf16 = jnp.bfloat16

    def perm_spec(d):
        return pl.BlockSpec((1, d, bm // d, k), lambda i: (i // per_seq, 0, i % per_seq, 0))

    return pl.pallas_call(
        _a_norm_kernel,
        grid=(m // bm,),
        in_specs=[pl.BlockSpec((bm, k), lambda i: (i, 0)), pl.BlockSpec((1, k), lambda i: (0, 0))],
        out_specs=[pl.BlockSpec((bm, k), lambda i: (i, 0)), perm_spec(4), perm_spec(16)],
        out_shape=[
            jax.ShapeDtypeStruct((m, k), bf16),
            jax.ShapeDtypeStruct((batch, 4, seq // 4, k), bf16),
            jax.ShapeDtypeStruct((batch, 16, seq // 16, k), bf16),
        ],
        scratch_shapes=[pltpu.VMEM((k // 128, bm, 128), jnp.float32)],
        compiler_params=_params(1),
        name="a_rmsnorm_permute",
    )(x2, g)


def _a_in_kernel(h_ref, w_ref, cs_ref, o_ref, wb_ref, *, blocks_per_section):
    @pl.when(pl.program_id(1) == 0)
    def _():
        rows = 256
        for c0 in range(w_ref.shape[0] // rows):
            rs = slice(c0 * rows, (c0 + 1) * rows)
            wb_ref[rs, :] = w_ref[rs, :].astype(wb_ref.dtype)

    acc = jnp.dot(h_ref[...], wb_ref[...], preferred_element_type=jnp.float32)
    is_q = pl.program_id(0) < blocks_per_section
    for hh in range(acc.shape[1] // HEAD_DIM):
        sl = slice(hh * HEAD_DIM, (hh + 1) * HEAD_DIM)
        a = acc[:, sl]
        ms = jnp.mean(a * a, axis=-1, keepdims=True)
        scale = jnp.where(is_q, lax.rsqrt(ms + EPS) * cs_ref[:, sl], 1.0)
        o_ref[:, sl] = (a * scale).astype(o_ref.dtype)


def _a_in(h, w, colscale, g, with_gate, bm=1024, bn=2048):
    m, k = h.shape
    per_section = ATT_WIDTH // bn
    n_blocks = (3 if with_gate else 2) * per_section
    group0 = 3 * g * per_section
    gate0 = 3 * N_GROUPS * per_section

    def col(j, i):
        sec, off = j // per_section, j % per_section
        return (0, jnp.where(sec == 2, gate0 + off, group0 + 2 * sec * per_section + off))

    return pl.pallas_call(
        functools.partial(_a_in_kernel, blocks_per_section=per_section),
        grid=(n_blocks, m // bm),
        in_specs=[
            pl.BlockSpec((bm, k), lambda j, i: (i, 0)),
            pl.BlockSpec((k, bn), col, pipeline_mode=pl.Buffered(1)),
            pl.BlockSpec((1, bn), col),
        ],
        out_specs=pl.BlockSpec((bm, bn), lambda j, i: (i, j)),
        out_shape=jax.ShapeDtypeStruct((m, n_blocks * bn), jnp.bfloat16),
        scratch_shapes=[pltpu.VMEM((k, bn), jnp.bfloat16)],
        compiler_params=_params(2),
        name=f"a_in_proj_qv_g{g}",
    )(h, w, colscale)


def _a_in_k_kernel(h_ref, w_ref, gain_ref, o_ref, wt_ref):
    @pl.when(pl.program_id(1) == 0)
    def _():
        rows = 256
        for c0 in range(w_ref.shape[0] // rows):
            rs = slice(c0 * rows, (c0 + 1) * rows)
            wt_ref[:, rs] = w_ref[rs, :].T.astype(wt_ref.dtype)

    acc = lax.dot_general(wt_ref[...], h_ref[...], (((1,), (1,)), ((), ())),
                          preferred_element_type=jnp.float32)
    bm = acc.shape[1]
    for hh in range(acc.shape[0] // HEAD_DIM):
        hs = slice(hh * HEAD_DIM, (hh + 1) * HEAD_DIM)
        a = acc[hs, :]
        ms = jnp.mean(a * a, axis=0, keepdims=True)
        an = a * lax.rsqrt(ms + EPS)
        for lb in range(bm // 128):
            ls = slice(lb * 128, (lb + 1) * 128)
            o_ref[hs, ls] = (an[:, ls] * gain_ref[0]).astype(o_ref.dtype)


def _a_in_k(h, w, gain_tile, g, bm=1024, bn=1024):
    m, k = h.shape
    per_section = ATT_WIDTH // bn
    k0 = (3 * g + 1) * per_section
    return pl.pallas_call(
        _a_in_k_kernel,
        grid=(per_section, m // bm),
        in_specs=[
            pl.BlockSpec((bm, k), lambda j, i: (i, 0)),
            pl.BlockSpec((k, bn), lambda j, i: (0, k0 + j)),
            pl.BlockSpec((1, HEAD_DIM, 128), lambda j, i: (g, 0, 0)),
        ],
        out_specs=pl.BlockSpec((bn, bm), lambda j, i: (j, i)),
        out_shape=jax.ShapeDtypeStruct((ATT_WIDTH, m), jnp.bfloat16),
        scratch_shapes=[pltpu.VMEM((bn, k), jnp.bfloat16)],
        compiler_params=_params(2),
        name=f"a_in_proj_kT_g{g}",
    )(h, w, gain_tile)


def _bias_kernel(table_ref, bucket_ref, o_ref):
    g = pl.program_id(0)
    hh = pl.program_id(1)
    bucket = bucket_ref[0]
    acc = jnp.full(bucket.shape, NEG, jnp.float32)
    for b in range(N_BUCKETS):
        acc = jnp.where(bucket == b, table_ref[b, g * N_HEADS + hh] * LOG2E, acc)
    col = lax.broadcasted_iota(jnp.int32, bucket.shape, 1)
    o_ref[0, 0, 0] = acc
    o_ref[0, 1, 0] = jnp.where(col < BLOCK, NEG, acc)


def _expand_bias(rel_bias, bucket):
    return pl.pallas_call(
        _bias_kernel,
        grid=(N_GROUPS, N_HEADS),
        in_specs=[
            pl.BlockSpec(memory_space=pltpu.SMEM),
            pl.BlockSpec((1, BLOCK, 2 * BLOCK), lambda g, h: (g, 0, 0)),
        ],
        out_specs=pl.BlockSpec((1, 2, 1, BLOCK, 2 * BLOCK), lambda g, h: (g, 0, h, 0, 0)),
        out_shape=jax.ShapeDtypeStruct((N_GROUPS, 2, N_HEADS, BLOCK, 2 * BLOCK), jnp.float32),
        compiler_params=_params(2),
        name="rel_bias_expand",
    )(rel_bias, bucket)


def _bucket_tiles():
    max_exact = N_BUCKETS // 2
    qi = jnp.arange(BLOCK)[:, None] + BLOCK
    kj = jnp.arange(2 * BLOCK)[None, :]
    step = qi - kj
    tiles = []
    for window, dilation in ATT_GROUPS:
        steps = window // dilation
        in_window = (step >= 0) & (step <= steps)
        dist = jnp.maximum(step, 0) * dilation
        is_small = dist < max_exact
        ratio = jnp.log(jnp.maximum(dist, 1).astype(jnp.float32) / max_exact) / math.log(MAX_DISTANCE / max_exact)
        large = max_exact + (ratio * (N_BUCKETS - max_exact)).astype(jnp.int32)
        large = jnp.minimum(large, N_BUCKETS - 1)
        bucket = jnp.where(is_small, dist, large)
        tiles.append(jnp.where(in_window, bucket, -1).astype(jnp.int32))
    return jnp.stack(tiles, axis=0)


def _attn_kernel(q_ref, kt_ref, ktp_ref, v_ref, vp_ref, bias_ref, o_ref, lse_ref, kk_ref, vv_ref):
    n = pl.program_id(1)
    span = ATT_QB * BLOCK
    kk_ref[:, 0:BLOCK] = ktp_ref[...]
    kk_ref[:, BLOCK:BLOCK + span] = kt_ref[...]
    vv_ref[0:BLOCK, :] = vp_ref[0]
    vv_ref[BLOCK:BLOCK + span, :] = v_ref[0]

    lane = lax.broadcasted_iota(jnp.int32, (BLOCK, BLOCK), 1)
    for sb in range(ATT_QB):
        rows = slice(sb * BLOCK, (sb + 1) * BLOCK)
        keys = slice(sb * BLOCK, (sb + 2) * BLOCK)
        variant = jnp.where(n == 0, 1, 0) if sb == 0 else 0
        lse_tile = jnp.zeros((BLOCK, BLOCK), jnp.float32)
        for hh in range(N_HEADS):
            sl = slice(hh * HEAD_DIM, (hh + 1) * HEAD_DIM)
            s = jnp.dot(q_ref[0, rows, sl], kk_ref[sl, keys], preferred_element_type=jnp.float32)
            s = s + bias_ref[0, variant, hh]
            m = jnp.max(s, axis=-1, keepdims=True)
            p = jnp.exp2(s - m)
            l = jnp.sum(p, axis=-1, keepdims=True)
            o = jnp.dot(p.astype(vv_ref.dtype), vv_ref[keys, sl], preferred_element_type=jnp.float32)
            o_ref[0, rows, sl] = (o / l).astype(o_ref.dtype)
            lse_tile = jnp.where(lane == hh, (m + jnp.log2(l)) * LN2, lse_tile)
        lse_ref[0, rows, :] = lse_tile


def _attention(qv, kt, bias, g, batch, seq):
    _, d = ATT_GROUPS[g]
    sub_len = seq // d
    span = ATT_QB * BLOCK
    nb = sub_len // span
    qv3 = qv.reshape(batch * d, sub_len, qv.shape[1])

    return pl.pallas_call(
        _attn_kernel,
        grid=(batch * d, nb),
        in_specs=[
            pl.BlockSpec((1, span, ATT_WIDTH), lambda s, n: (s, n, 0)),
            pl.BlockSpec((ATT_WIDTH, span), lambda s, n: (0, s * nb + n)),
            pl.BlockSpec((ATT_WIDTH, BLOCK),
                         lambda s, n: (0, jnp.maximum((s * nb + n) * ATT_QB - 1, 0))),
            pl.BlockSpec((1, span, ATT_WIDTH), lambda s, n: (s, n, 1)),
            pl.BlockSpec((1, BLOCK, ATT_WIDTH), lambda s, n: (s, jnp.maximum(n * ATT_QB - 1, 0), 1)),
            pl.BlockSpec((1, 2, N_HEADS, BLOCK, 2 * BLOCK), lambda s, n: (g, 0, 0, 0, 0)),
        ],
        out_specs=[
            pl.BlockSpec((1, span, ATT_WIDTH), lambda s, n: (s, n, 0)),
            pl.BlockSpec((1, span, BLOCK), lambda s, n: (s, n, 0)),
        ],
        out_shape=[
            jax.ShapeDtypeStruct((batch * d, sub_len, ATT_WIDTH), jnp.bfloat16),
            jax.ShapeDtypeStruct((batch * d, sub_len, BLOCK), jnp.float32),
        ],
        scratch_shapes=[
            pltpu.VMEM((ATT_WIDTH, BLOCK + span), jnp.bfloat16),
            pltpu.VMEM((BLOCK + span, ATT_WIDTH), jnp.bfloat16),
        ],
        compiler_params=_params(2),
        name=f"dilated_attn_g{g}",
    )(qv3, kt, kt, qv3, qv3, bias)


def _a_out_kernel(o0_ref, o1_ref, o2_ref, l0_ref, l1_ref, l2_ref, z_ref, x_ref, w_ref,
                  out_ref, l1n_ref, l2n_ref, t1_ref, t2_ref, y_ref):
    bm = x_ref.shape[0]
    for r in range(4):
        l1n_ref[pl.ds(r, bm // 4, stride=4), :] = l1_ref[0, r]
    for r in range(16):
        l2n_ref[pl.ds(r, bm // 16, stride=16), :] = l2_ref[0, r]
    l0, l1, l2 = l0_ref[...], l1n_ref[...], l2n_ref[...]
    mx = jnp.maximum(jnp.maximum(l0, l1), l2)
    e0, e1, e2 = jnp.exp(l0 - mx), jnp.exp(l1 - mx), jnp.exp(l2 - mx)
    den = e0 + e1 + e2
    w0, w1, w2 = e0 / den, e1 / den, e2 / den
    acc = x_ref[...]
    for hh in range(N_HEADS):
        sl = slice(hh * HEAD_DIM, (hh + 1) * HEAD_DIM)
        for r in range(4):
            t1_ref[hh, pl.ds(r, bm // 4, stride=4), :] = o1_ref[0, r, :, sl].astype(jnp.float32)
        for r in range(16):
            t2_ref[hh, pl.ds(r, bm // 16, stride=16), :] = o2_ref[0, r, :, sl].astype(jnp.float32)
        o = (w0[:, hh:hh + 1] * o0_ref[:, sl].astype(jnp.float32)
             + w1[:, hh:hh + 1] * t1_ref[hh]
             + w2[:, hh:hh + 1] * t2_ref[hh])
        y_ref[:, sl] = (o * jax.nn.silu(z_ref[:, sl].astype(jnp.float32))).astype(y_ref.dtype)
        if hh % K_CHUNK_HEADS == K_CHUNK_HEADS - 1:
            ks = slice((hh + 1 - K_CHUNK_HEADS) * HEAD_DIM, (hh + 1) * HEAD_DIM)
            acc = acc + jnp.dot(y_ref[:, ks], w_ref[ks, :], preferred_element_type=jnp.float32)
    out_ref[...] = acc


def _a_out(os, lses, p0, x2, w_out, batch, seq, bm=256):
    m = x2.shape[0]
    per_seq = seq // bm
    z_block = 2
    row = lambda i: (i, 0)

    def perm_spec(d, width):
        return pl.BlockSpec((1, d, bm // d, width), lambda i: (i // per_seq, 0, i % per_seq, 0))

    def perm_view(a, d):
        return a.reshape(batch, d, seq // d, a.shape[-1])

    return pl.pallas_call(
        _a_out_kernel,
        grid=(m // bm,),
        in_specs=[
            pl.BlockSpec((bm, ATT_WIDTH), row), perm_spec(4, ATT_WIDTH), perm_spec(16, ATT_WIDTH),
            pl.BlockSpec((bm, BLOCK), row), perm_spec(4, BLOCK), perm_spec(16, BLOCK),
            pl.BlockSpec((bm, ATT_WIDTH), lambda i: (i, z_block)),
            pl.BlockSpec((bm, D_MODEL), row),
            pl.BlockSpec((ATT_WIDTH, D_MODEL), lambda i: (0, 0)),
        ],
        out_specs=pl.BlockSpec((bm, D_MODEL), row),
        out_shape=jax.ShapeDtypeStruct((m, D_MODEL), jnp.float32),
        scratch_shapes=[
            pltpu.VMEM((bm, BLOCK), jnp.float32),
            pltpu.VMEM((bm, BLOCK), jnp.float32),
            pltpu.VMEM((N_HEADS, bm, HEAD_DIM), jnp.float32),
            pltpu.VMEM((N_HEADS, bm, HEAD_DIM), jnp.float32),
            pltpu.VMEM((bm, ATT_WIDTH), jnp.bfloat16),
        ],
        compiler_params=_params(1),
        name="a_merge_out_proj",
    )(os[0].reshape(m, ATT_WIDTH), perm_view(os[1], 4), perm_view(os[2], 16),
      lses[0].reshape(m, BLOCK), perm_view(lses[1], 4), perm_view(lses[2], 16),
      p0, x2, w_out)


def _b_in_kernel(x_ref, g_ref, wa_ref, wg_ref, wz_ref, ba_ref, bg_ref, bz_ref,
                 u_ref, sz_ref, h_ref):
    @pl.when(pl.program_id(1) == 0)
    def _():
        _rmsnorm_to(h_ref, x_ref, g_ref, 128)

    h = h_ref[...]
    a = jnp.dot(h, wa_ref[...], preferred_element_type=jnp.float32) + ba_ref[...]
    ga = jnp.dot(h, wg_ref[...], preferred_element_type=jnp.float32) + bg_ref[...]
    u_ref[...] = a * jax.nn.sigmoid(ga)
    z = jnp.dot(h, wz_ref[...], preferred_element_type=jnp.float32) + bz_ref[...]
    sz_ref[...] = jax.nn.silu(z).astype(sz_ref.dtype)


def _b_in(x2, g, w, b, bm=1024, bn=512):
    m, k = x2.shape
    ch = w.shape[1] // 3
    nj = ch // bn

    def wspec(t):
        return pl.BlockSpec((k, bn), lambda i, j: (0, t * nj + j))

    def bspec(t):
        return pl.BlockSpec((1, bn), lambda i, j: (0, t * nj + j))

    return pl.pallas_call(
        _b_in_kernel,
        grid=(m // bm, nj),
        in_specs=[
            pl.BlockSpec((bm, k), lambda i, j: (i, 0)),
            pl.BlockSpec((1, k), lambda i, j: (0, 0)),
            wspec(0), wspec(1), wspec(2), bspec(0), bspec(1), bspec(2),
        ],
        out_specs=[
            pl.BlockSpec((bm, bn), lambda i, j: (i, j)),
            pl.BlockSpec((bm, bn), lambda i, j: (i, j)),
        ],
        out_shape=[
            jax.ShapeDtypeStruct((m, ch), jnp.float32),
            jax.ShapeDtypeStruct((m, ch), jnp.bfloat16),
        ],
        scratch_shapes=[pltpu.VMEM((bm, k), jnp.bfloat16)],
        compiler_params=_params(2),
        name="b_in_proj_glu",
    )(x2, g, w, w, w, b, b, b)


def _b_out_kernel(u_ref, halo_ref, sz_ref, x_ref, cw_ref, cb_ref, lg_ref, lb_ref,
                  w_ref, bo_ref, out_ref, ext_ref, conv_ref, y_ref, *, blocks_per_seq):
    bm = u_ref.shape[0]
    n_slabs = u_ref.shape[1] // 128
    i = pl.program_id(0)
    seq_start = (i % blocks_per_seq) == 0

    for c in range(n_slabs):
        sl = slice(c * 128, (c + 1) * 128)
        ext_ref[c, CONV_HALO:CONV_HALO + bm, :] = u_ref[:, sl]

    @pl.when(seq_start)
    def _():
        ext_ref[:, 0:CONV_HALO, :] = jnp.zeros((n_slabs, CONV_HALO, 128), jnp.float32)

    @pl.when(jnp.logical_not(seq_start))
    def _():
        for c in range(n_slabs):
            ext_ref[c, 0:CONV_HALO, :] = halo_ref[:, c * 128:(c + 1) * 128]

    first_tap = CONV_HALO - (CONV_WIDTH - 1)
    chunks = bm // CONV_ROWS

    def conv_body(t, carry):
        c = t // chunks
        r0 = pl.multiple_of((t % chunks) * CONV_ROWS, CONV_ROWS)
        acc = jnp.zeros((CONV_ROWS, 128), jnp.float32)
        for kk in range(CONV_WIDTH):
            acc = acc + cw_ref[c, kk:kk + 1, :] * ext_ref[c, pl.ds(r0 + first_tap + kk, CONV_ROWS), :]
        conv_ref[c, pl.ds(r0, CONV_ROWS), :] = acc
        return carry

    lax.fori_loop(0, n_slabs * chunks, conv_body, 0, unroll=4)

    tot = jnp.zeros((bm, 128), jnp.float32)
    for c in range(n_slabs):
        sl = slice(c * 128, (c + 1) * 128)
        tot = tot + (conv_ref[c] + cb_ref[:, sl])
    mu = jnp.sum(tot, axis=-1, keepdims=True) / (n_slabs * 128)
    sq = jnp.zeros((bm, 128), jnp.float32)
    for c in range(n_slabs):
        sl = slice(c * 128, (c + 1) * 128)
        dv = conv_ref[c] + cb_ref[:, sl] - mu
        sq = sq + dv * dv
    var = jnp.sum(sq, axis=-1, keepdims=True) / (n_slabs * 128)
    inv = lax.rsqrt(var + EPS)
    acc = x_ref[...] + bo_ref[...]
    for c in range(n_slabs):
        sl = slice(c * 128, (c + 1) * 128)
        uf = (conv_ref[c] + cb_ref[:, sl] - mu) * inv * lg_ref[:, sl] + lb_ref[:, sl]
        y_ref[:, sl] = (jax.nn.silu(uf) * sz_ref[:, sl].astype(jnp.float32)).astype(y_ref.dtype)
        if c % K_CHUNK_HEADS == K_CHUNK_HEADS - 1:
            ks = slice((c + 1 - K_CHUNK_HEADS) * 128, (c + 1) * 128)
            acc = acc + jnp.dot(y_ref[:, ks], w_ref[ks, :], preferred_element_type=jnp.float32)
    out_ref[...] = acc


def _b_out(u, sz, x2, conv_w_slabs, conv_b, ln_g, ln_b, w_out, b_out, seq, bm=256):
    m, ch = u.shape
    n_slabs = ch // 128
    halo_blocks = bm // CONV_HALO
    row = lambda i: (i, 0)
    const2 = lambda i: (0, 0)
    return pl.pallas_call(
        functools.partial(_b_out_kernel, blocks_per_seq=seq // bm),
        grid=(m // bm,),
        in_specs=[
            pl.BlockSpec((bm, ch), row),
            pl.BlockSpec((CONV_HALO, ch), lambda i: (jnp.maximum(i * halo_blocks - 1, 0), 0)),
            pl.BlockSpec((bm, ch), row),
            pl.BlockSpec((bm, D_MODEL), row),
            pl.BlockSpec((n_slabs, CONV_HALO, 128), lambda i: (0, 0, 0)),
            pl.BlockSpec((1, ch), const2), pl.BlockSpec((1, ch), const2),
            pl.BlockSpec((1, ch), const2),
            pl.BlockSpec((ch, D_MODEL), const2),
            pl.BlockSpec((1, D_MODEL), const2),
        ],
        out_specs=pl.BlockSpec((bm, D_MODEL), row),
        out_shape=jax.ShapeDtypeStruct((m, D_MODEL), jnp.float32),
        scratch_shapes=[
            pltpu.VMEM((n_slabs, CONV_HALO + bm, 128), jnp.float32),
            pltpu.VMEM((n_slabs, bm, 128), jnp.float32),
            pltpu.VMEM((bm, ch), jnp.bfloat16),
        ],
        compiler_params=_params(1),
        name="b_conv_ln_out_proj",
    )(u, u, sz, x2, conv_w_slabs, conv_b, ln_g, ln_b, w_out, b_out)


def kernel(x, norm_g, rel_bias, a_w_in, a_q_gain, a_k_gain, a_w_out, b_w_in, b_b_in,
           b_conv_w, b_conv_b, b_ln_g, b_ln_b, b_w_out, b_b_out):
    batch, seq, dm = x.shape
    m = batch * seq
    bf16 = jnp.bfloat16
    x2 = x.reshape(m, dm)

    ones = jnp.ones((ATT_WIDTH,), jnp.float32)
    cols = []
    for g in range(N_GROUPS):
        cols += [jnp.tile(a_q_gain[0, g], N_HEADS) * (HEAD_DIM ** -0.5 * LOG2E), ones, ones]
    cols.append(ones)
    colscale = jnp.concatenate(cols).reshape(1, A_COLS)
    k_gain_tile = jnp.broadcast_to(a_k_gain[0][:, :, None], (N_GROUPS, HEAD_DIM, 128))

    w_in = a_w_in[0]
    hs = _a_norm(x2, norm_g[0:1], batch, seq)
    bias = _expand_bias(rel_bias, _bucket_tiles())
    os, lses, p0 = [], [], None
    for g in range(N_GROUPS):
        hg = hs[g].reshape(m, dm)
        qv = _a_in(hg, w_in, colscale, g, with_gate=(g == 0))
        kt = _a_in_k(hg, w_in, k_gain_tile, g)
        if g == 0:
            p0 = qv
        o, lse = _attention(qv, kt, bias, g, batch, seq)
        os.append(o)
        lses.append(lse)
    x2 = _a_out(os, lses, p0, x2, a_w_out[0].astype(bf16), batch, seq)

    u, sz = _b_in(x2, norm_g[1:2], b_w_in[0].astype(bf16), b_b_in[0:1])
    ch = u.shape[1]
    cw = jnp.pad(b_conv_w[0], ((0, CONV_HALO - CONV_WIDTH), (0, 0)))
    cw = cw.reshape(CONV_HALO, ch // 128, 128).transpose(1, 0, 2)
    x2 = _b_out(u, sz, x2, cw, b_conv_b[0:1], b_ln_g[0:1], b_ln_b[0:1],
                b_w_out[0].astype(bf16), b_b_out[0:1], seq)
    return x2.reshape(batch, seq, dm)
```

```python
import functools
import math

import jax
import jax.numpy as jnp
from jax import lax
from jax.experimental import pallas as pl
from jax.experimental.pallas import tpu as pltpu

D_MODEL = 2048
HEAD_DIM = 128
N_HEADS = 16
ATT_GROUPS = ((128, 1), (512, 4), (2048, 16))
N_GROUPS = 3
ATT_WIDTH = N_HEADS * HEAD_DIM
A_COLS = 3 * N_GROUPS * ATT_WIDTH + ATT_WIDTH
BLOCK = 128
N_BUCKETS = 32
MAX_DISTANCE = 2048
CONV_WIDTH = 31
EPS = 1e-6
NEG = -1e30
LOG2E = math.log2(math.e)
LN2 = math.log(2.0)

VMEM_LIMIT_BYTES = 56 * 1024 * 1024
CONV_HALO = 32
CONV_ROWS = 64
ATT_QB = 4
K_CHUNK_HEADS = 2


def _params(n_axes):
    return pltpu.CompilerParams(
        dimension_semantics=("arbitrary",) * n_axes,
        vmem_limit_bytes=VMEM_LIMIT_BYTES)


def _rmsnorm_to(h_ref, x_ref, g_ref, rows):
    n_chunks = x_ref.shape[0] // rows

    def body(c, carry):
        r0 = pl.multiple_of(c * rows, rows)
        xc = x_ref[pl.ds(r0, rows), :]
        ms = jnp.mean(xc * xc, axis=-1, keepdims=True)
        h_ref[pl.ds(r0, rows), :] = (xc * lax.rsqrt(ms + EPS) * g_ref[...]).astype(h_ref.dtype)
        return carry

    lax.fori_loop(0, n_chunks, body, 0)


def _a_norm_kernel(x_ref, g_ref, hn_ref, h4_ref, h16_ref, slab_ref):
    bm = x_ref.shape[0]
    n_slabs = x_ref.shape[1] // 128
    rows = 128
    for c0 in range(bm // rows):
        rs = slice(c0 * rows, (c0 + 1) * rows)
        xc = x_ref[rs, :]
        ms = jnp.mean(xc * xc, axis=-1, keepdims=True)
        hn = xc * lax.rsqrt(ms + EPS) * g_ref[...]
        hn_ref[rs, :] = hn.astype(hn_ref.dtype)
        for c in range(n_slabs):
            slab_ref[c, rs, :] = hn[:, c * 128:(c + 1) * 128]
    for c in range(n_slabs):
        sl = slice(c * 128, (c + 1) * 128)
        for r in range(4):
            h4_ref[0, r, :, sl] = slab_ref[c, pl.ds(r, bm // 4, stride=4), :].astype(h4_ref.dtype)
        for r in range(16):
            h16_ref[0, r, :, sl] = slab_ref[c, pl.ds(r, bm // 16, stride=16), :].astype(h16_ref.dtype)


def _a_norm(x2, g, batch, seq, bm=512):
    m, k = x2.shape
    per_seq = seq // bm
    bf16 = jnp.bfloat16

    def perm_spec(d):
        return pl.BlockSpec((1, d, bm // d, k), lambda i: (i // per_seq, 0, i % per_seq, 0))

    return pl.pallas_call(
        _a_norm_kernel,
        grid=(m // bm,),
        in_specs=[pl.BlockSpec((bm, k), lambda i: (i, 0)), pl.BlockSpec((1, k), lambda i: (0, 0))],
        out_specs=[pl.BlockSpec((bm, k), lambda i: (i, 0)), perm_spec(4), perm_spec(16)],
        out_shape=[
            jax.ShapeDtypeStruct((m, k), bf16),
            jax.ShapeDtypeStruct((batch, 4, seq // 4, k), bf16),
            jax.ShapeDtypeStruct((batch, 16, seq // 16, k), bf16),
        ],
        scratch_shapes=[pltpu.VMEM((k // 128, bm, 128), jnp.float32)],
        compiler_params=_params(1),
        name="a_rmsnorm_permute",
    )(x2, g)


def _a_in_kernel(h_ref, w_ref, cs_ref, o_ref, wb_ref, *, blocks_per_section):
    @pl.when(pl.program_id(1) == 0)
    def _():
        rows = 256
        for c0 in range(w_ref.shape[0] // rows):
            rs = slice(c0 * rows, (c0 + 1) * rows)
            wb_ref[rs, :] = w_ref[rs, :].astype(wb_ref.dtype)

    acc = jnp.dot(h_ref[...], wb_ref[...], preferred_element_type=jnp.float32)
    is_q = pl.program_id(0) < blocks_per_section
    for hh in range(acc.shape[1] // HEAD_DIM):
        sl = slice(hh * HEAD_DIM, (hh + 1) * HEAD_DIM)
        a = acc[:, sl]
        ms = jnp.mean(a * a, axis=-1, keepdims=True)
        scale = jnp.where(is_q, lax.rsqrt(ms + EPS) * cs_ref[:, sl], 1.0)
        o_ref[:, sl] = (a * scale).astype(o_ref.dtype)


def _a_in(h, w, colscale, g, with_gate, bm=1024, bn=2048):
    m, k = h.shape
    per_section = ATT_WIDTH // bn
    n_blocks = (3 if with_gate else 2) * per_section
    group0 = 3 * g * per_section
    gate0 = 3 * N_GROUPS * per_section

    def col(j, i):
        sec, off = j // per_section, j % per_section
        return (0, jnp.where(sec == 2, gate0 + off, group0 + 2 * sec * per_section + off))

    return pl.pallas_call(
        functools.partial(_a_in_kernel, blocks_per_section=per_section),
        grid=(n_blocks, m // bm),
        in_specs=[
            pl.BlockSpec((bm, k), lambda j, i: (i, 0)),
            pl.BlockSpec((k, bn), col, pipeline_mode=pl.Buffered(1)),
            pl.BlockSpec((1, bn), col),
        ],
        out_specs=pl.BlockSpec((bm, bn), lambda j, i: (i, j)),
        out_shape=jax.ShapeDtypeStruct((m, n_blocks * bn), jnp.bfloat16),
        scratch_shapes=[pltpu.VMEM((k, bn), jnp.bfloat16)],
        compiler_params=_params(2),
        name=f"a_in_proj_qv_g{g}",
    )(h, w, colscale)


def _a_in_k_kernel(h_ref, w_ref, gain_ref, o_ref, wt_ref):
    @pl.when(pl.program_id(1) == 0)
    def _():
        rows = 256
        for c0 in range(w_ref.shape[0] // rows):
            rs = slice(c0 * rows, (c0 + 1) * rows)
            wt_ref[:, rs] = w_ref[rs, :].T.astype(wt_ref.dtype)

    acc = lax.dot_general(wt_ref[...], h_ref[...], (((1,), (1,)), ((), ())),
                          preferred_element_type=jnp.float32)
    bm = acc.shape[1]
    for hh in range(acc.shape[0] // HEAD_DIM):
        hs = slice(hh * HEAD_DIM, (hh + 1) * HEAD_DIM)
        a = acc[hs, :]
        ms = jnp.mean(a * a, axis=0, keepdims=True)
        an = a * lax.rsqrt(ms + EPS)
        for lb in range(bm // 128):
            ls = slice(lb * 128, (lb + 1) * 128)
            o_ref[hs, ls] = (an[:, ls] * gain_ref[0]).astype(o_ref.dtype)


def _a_in_k(h, w, gain_tile, g, bm=1024, bn=2048):
    m, k = h.shape
    per_section = ATT_WIDTH // bn
    k0 = (3 * g + 1) * per_section
    return pl.pallas_call(
        _a_in_k_kernel,
        grid=(per_section, m // bm),
        in_specs=[
            pl.BlockSpec((bm, k), lambda j, i: (i, 0)),
            pl.BlockSpec((k, bn), lambda j, i: (0, k0 + j), pipeline_mode=pl.Buffered(1)),
            pl.BlockSpec((1, HEAD_DIM, 128), lambda j, i: (g, 0, 0)),
        ],
        out_specs=pl.BlockSpec((bn, bm), lambda j, i: (j, i)),
        out_shape=jax.ShapeDtypeStruct((ATT_WIDTH, m), jnp.bfloat16),
        scratch_shapes=[pltpu.VMEM((bn, k), jnp.bfloat16)],
        compiler_params=_params(2),
        name=f"a_in_proj_kT_g{g}",
    )(h, w, gain_tile)


def _bias_kernel(table_ref, bucket_ref, o_ref):
    g = pl.program_id(0)
    hh = pl.program_id(1)
    bucket = bucket_ref[0]
    acc = jnp.full(bucket.shape, NEG, jnp.float32)
    for b in range(N_BUCKETS):
        acc = jnp.where(bucket == b, table_ref[b, g * N_HEADS + hh] * LOG2E, acc)
    col = lax.broadcasted_iota(jnp.int32, bucket.shape, 1)
    o_ref[0, 0, 0] = acc
    o_ref[0, 1, 0] = jnp.where(col < BLOCK, NEG, acc)


def _expand_bias(rel_bias, bucket):
    return pl.pallas_call(
        _bias_kernel,
        grid=(N_GROUPS, N_HEADS),
        in_specs=[
            pl.BlockSpec(memory_space=pltpu.SMEM),
            pl.BlockSpec((1, BLOCK, 2 * BLOCK), lambda g, h: (g, 0, 0)),
        ],
        out_specs=pl.BlockSpec((1, 2, 1, BLOCK, 2 * BLOCK), lambda g, h: (g, 0, h, 0, 0)),
        out_shape=jax.ShapeDtypeStruct((N_GROUPS, 2, N_HEADS, BLOCK, 2 * BLOCK), jnp.float32),
        compiler_params=_params(2),
        name="rel_bias_expand",
    )(rel_bias, bucket)


def _bucket_tiles():
    max_exact = N_BUCKETS // 2
    qi = jnp.arange(BLOCK)[:, None] + BLOCK
    kj = jnp.arange(2 * BLOCK)[None, :]
    step = qi - kj
    tiles = []
    for window, dilation in ATT_GROUPS:
        steps = window // dilation
        in_window = (step >= 0) & (step <= steps)
        dist = jnp.maximum(step, 0) * dilation
        is_small = dist < max_exact
        ratio = jnp.log(jnp.maximum(dist, 1).astype(jnp.float32) / max_exact) / math.log(MAX_DISTANCE / max_exact)
        large = max_exact + (ratio * (N_BUCKETS - max_exact)).astype(jnp.int32)
        large = jnp.minimum(large, N_BUCKETS - 1)
        bucket = jnp.where(is_small, dist, large)
        tiles.append(jnp.where(in_window, bucket, -1).astype(jnp.int32))
    return jnp.stack(tiles, axis=0)


def _attn_kernel(q_ref, kt_ref, ktp_ref, v_ref, vp_ref, bias_ref, o_ref, lse_ref, kk_ref, vv_ref):
    n = pl.program_id(1)
    span = ATT_QB * BLOCK
    kk_ref[:, 0:BLOCK] = ktp_ref[...]
    kk_ref[:, BLOCK:BLOCK + span] = kt_ref[...]
    vv_ref[0:BLOCK, :] = vp_ref[0]
    vv_ref[BLOCK:BLOCK + span, :] = v_ref[0]

    lane = lax.broadcasted_iota(jnp.int32, (BLOCK, BLOCK), 1)
    for sb in range(ATT_QB):
        rows = slice(sb * BLOCK, (sb + 1) * BLOCK)
        keys = slice(sb * BLOCK, (sb + 2) * BLOCK)
        variant = jnp.where(n == 0, 1, 0) if sb == 0 else 0
        lse_tile = jnp.zeros((BLOCK, BLOCK), jnp.float32)
        for hh in range(N_HEADS):
            sl = slice(hh * HEAD_DIM, (hh + 1) * HEAD_DIM)
            s = jnp.dot(q_ref[0, rows, sl], kk_ref[sl, keys], preferred_element_type=jnp.float32)
            s = s + bias_ref[0, variant, hh]
            m = jnp.max(s, axis=-1, keepdims=True)
            p = jnp.exp2(s - m)
            l = jnp.sum(p, axis=-1, keepdims=True)
            o = jnp.dot(p.astype(vv_ref.dtype), vv_ref[keys, sl], preferred_element_type=jnp.float32)
            o_ref[0, rows, sl] = (o / l).astype(o_ref.dtype)
            lse_tile = jnp.where(lane == hh, (m + jnp.log2(l)) * LN2, lse_tile)
        lse_ref[0, rows, :] = lse_tile


def _attention(qv, kt, bias, g, batch, seq):
    _, d = ATT_GROUPS[g]
    sub_len = seq // d
    span = ATT_QB * BLOCK
    nb = sub_len // span
    qv3 = qv.reshape(batch * d, sub_len, qv.shape[1])

    return pl.pallas_call(
        _attn_kernel,
        grid=(batch * d, nb),
        in_specs=[
            pl.BlockSpec((1, span, ATT_WIDTH), lambda s, n: (s, n, 0)),
            pl.BlockSpec((ATT_WIDTH, span), lambda s, n: (0, s * nb + n)),
            pl.BlockSpec((ATT_WIDTH, BLOCK),
                         lambda s, n: (0, jnp.maximum((s * nb + n) * ATT_QB - 1, 0))),
            pl.BlockSpec((1, span, ATT_WIDTH), lambda s, n: (s, n, 1)),
            pl.BlockSpec((1, BLOCK, ATT_WIDTH), lambda s, n: (s, jnp.maximum(n * ATT_QB - 1, 0), 1)),
            pl.BlockSpec((1, 2, N_HEADS, BLOCK, 2 * BLOCK), lambda s, n: (g, 0, 0, 0, 0)),
        ],
        out_specs=[
            pl.BlockSpec((1, span, ATT_WIDTH), lambda s, n: (s, n, 0)),
            pl.BlockSpec((1, span, BLOCK), lambda s, n: (s, n, 0)),
        ],
        out_shape=[
            jax.ShapeDtypeStruct((batch * d, sub_len, ATT_WIDTH), jnp.bfloat16),
            jax.ShapeDtypeStruct((batch * d, sub_len, BLOCK), jnp.float32),
        ],
        scratch_shapes=[
            pltpu.VMEM((ATT_WIDTH, BLOCK + span), jnp.bfloat16),
            pltpu.VMEM((BLOCK + span, ATT_WIDTH), jnp.bfloat16),
        ],
        compiler_params=_params(2),
        name=f"dilated_attn_g{g}",
    )(qv3, kt, kt, qv3, qv3, bias)


def _a_out_kernel(o0_ref, o1_ref, o2_ref, l0_ref, l1_ref, l2_ref, z_ref, x_ref, w_ref,
                  out_ref, l1n_ref, l2n_ref, t1_ref, t2_ref, y_ref):
    bm = x_ref.shape[0]
    for r in range(4):
        l1n_ref[pl.ds(r, bm // 4, stride=4), :] = l1_ref[0, r]
    for r in range(16):
        l2n_ref[pl.ds(r, bm // 16, stride=16), :] = l2_ref[0, r]
    l0, l1, l2 = l0_ref[...], l1n_ref[...], l2n_ref[...]
    mx = jnp.maximum(jnp.maximum(l0, l1), l2)
    e0, e1, e2 = jnp.exp(l0 - mx), jnp.exp(l1 - mx), jnp.exp(l2 - mx)
    den = e0 + e1 + e2
    w0, w1, w2 = e0 / den, e1 / den, e2 / den
    acc = x_ref[...]
    for hh in range(N_HEADS):
        sl = slice(hh * HEAD_DIM, (hh + 1) * HEAD_DIM)
        for r in range(4):
            t1_ref[hh, pl.ds(r, bm // 4, stride=4), :] = o1_ref[0, r, :, sl].astype(jnp.float32)
        for r in range(16):
            t2_ref[hh, pl.ds(r, bm // 16, stride=16), :] = o2_ref[0, r, :, sl].astype(jnp.float32)
        o = (w0[:, hh:hh + 1] * o0_ref[:, sl].astype(jnp.float32)
             + w1[:, hh:hh + 1] * t1_ref[hh]
             + w2[:, hh:hh + 1] * t2_ref[hh])
        y_ref[:, sl] = (o * jax.nn.silu(z_ref[:, sl].astype(jnp.float32))).astype(y_ref.dtype)
        if hh % K_CHUNK_HEADS == K_CHUNK_HEADS - 1:
            ks = slice((hh + 1 - K_CHUNK_HEADS) * HEAD_DIM, (hh + 1) * HEAD_DIM)
            acc = acc + jnp.dot(y_ref[:, ks], w_ref[ks, :], preferred_element_type=jnp.float32)
    out_ref[...] = acc


def _a_out(os, lses, p0, x2, w_out, batch, seq, bm=256):
    m = x2.shape[0]
    per_seq = seq // bm
    z_block = 2
    row = lambda i: (i, 0)

    def perm_spec(d, width):
        return pl.BlockSpec((1, d, bm // d, width), lambda i: (i // per_seq, 0, i % per_seq, 0))

    def perm_view(a, d):
        return a.reshape(batch, d, seq // d, a.shape[-1])

    return pl.pallas_call(
        _a_out_kernel,
        grid=(m // bm,),
        in_specs=[
            pl.BlockSpec((bm, ATT_WIDTH), row), perm_spec(4, ATT_WIDTH), perm_spec(16, ATT_WIDTH),
            pl.BlockSpec((bm, BLOCK), row), perm_spec(4, BLOCK), perm_spec(16, BLOCK),
            pl.BlockSpec((bm, ATT_WIDTH), lambda i: (i, z_block)),
            pl.BlockSpec((bm, D_MODEL), row),
            pl.BlockSpec((ATT_WIDTH, D_MODEL), lambda i: (0, 0)),
        ],
        out_specs=pl.BlockSpec((bm, D_MODEL), row),
        out_shape=jax.ShapeDtypeStruct((m, D_MODEL), jnp.float32),
        scratch_shapes=[
            pltpu.VMEM((bm, BLOCK), jnp.float32),
            pltpu.VMEM((bm, BLOCK), jnp.float32),
            pltpu.VMEM((N_HEADS, bm, HEAD_DIM), jnp.float32),
            pltpu.VMEM((N_HEADS, bm, HEAD_DIM), jnp.float32),
            pltpu.VMEM((bm, ATT_WIDTH), jnp.bfloat16),
        ],
        compiler_params=_params(1),
        name="a_merge_out_proj",
    )(os[0].reshape(m, ATT_WIDTH), perm_view(os[1], 4), perm_view(os[2], 16),
      lses[0].reshape(m, BLOCK), perm_view(lses[1], 4), perm_view(lses[2], 16),
      p0, x2, w_out)


def _b_in_kernel(x_ref, g_ref, w_ref, b_ref, u_ref, sz_ref, h_ref, *, bn):
    bm, ch = u_ref.shape
    rows = 128
    for r0 in range(0, bm, rows):
        rs = slice(r0, r0 + rows)
        xc = x_ref[rs, :]
        ms = jnp.mean(xc * xc, axis=-1, keepdims=True)
        h_ref[rs, :] = (xc * lax.rsqrt(ms + EPS) * g_ref[...]).astype(h_ref.dtype)
    h = h_ref[...]
    for c0 in range(0, ch, bn):
        ca, cg, cz = (slice(s * ch + c0, s * ch + c0 + bn) for s in range(3))
        cs = slice(c0, c0 + bn)
        a = jnp.dot(h, w_ref[:, ca], preferred_element_type=jnp.float32) + b_ref[:, ca]
        ga = jnp.dot(h, w_ref[:, cg], preferred_element_type=jnp.float32) + b_ref[:, cg]
        u_ref[:, cs] = a * jax.nn.sigmoid(ga)
        z = jnp.dot(h, w_ref[:, cz], preferred_element_type=jnp.float32) + b_ref[:, cz]
        sz_ref[:, cs] = jax.nn.silu(z).astype(sz_ref.dtype)


def _b_in(x2, g, w, b, bm=256, bn=512):
    m, k = x2.shape
    ch = w.shape[1] // 3
    row = lambda i: (i, 0)
    const2 = lambda i: (0, 0)
    return pl.pallas_call(
        functools.partial(_b_in_kernel, bn=bn),
        grid=(m // bm,),
        in_specs=[
            pl.BlockSpec((bm, k), row),
            pl.BlockSpec((1, k), const2),
            pl.BlockSpec(w.shape, const2, pipeline_mode=pl.Buffered(1)),
            pl.BlockSpec(b.shape, const2),
        ],
        out_specs=[pl.BlockSpec((bm, ch), row), pl.BlockSpec((bm, ch), row)],
        out_shape=[
            jax.ShapeDtypeStruct((m, ch), jnp.float32),
            jax.ShapeDtypeStruct((m, ch), jnp.bfloat16),
        ],
        scratch_shapes=[pltpu.VMEM((bm, k), jnp.bfloat16)],
        compiler_params=_params(1),
        name="b_in_proj_glu",
    )(x2, g, w, b)


def _b_out_kernel(u_ref, halo_ref, sz_ref, x_ref, cw_ref, cb_ref, lg_ref, lb_ref,
                  w_ref, bo_ref, out_ref, ext_ref, conv_ref, y_ref, *, blocks_per_seq):
    bm = u_ref.shape[0]
    n_slabs = u_ref.shape[1] // 128
    i = pl.program_id(0)
    seq_start = (i % blocks_per_seq) == 0

    for c in range(n_slabs):
        sl = slice(c * 128, (c + 1) * 128)
        ext_ref[c, CONV_HALO:CONV_HALO + bm, :] = u_ref[:, sl]

    @pl.when(seq_start)
    def _():
        ext_ref[:, 0:CONV_HALO, :] = jnp.zeros((n_slabs, CONV_HALO, 128), jnp.float32)

    @pl.when(jnp.logical_not(seq_start))
    def _():
        for c in range(n_slabs):
            ext_ref[c, 0:CONV_HALO, :] = halo_ref[:, c * 128:(c + 1) * 128]

    first_tap = CONV_HALO - (CONV_WIDTH - 1)
    chunks = bm // CONV_ROWS

    def conv_body(t, carry):
        c = t // chunks
        r0 = pl.multiple_of((t % chunks) * CONV_ROWS, CONV_ROWS)
        acc = jnp.zeros((CONV_ROWS, 128), jnp.float32)
        for kk in range(CONV_WIDTH):
            acc = acc + cw_ref[c, kk:kk + 1, :] * ext_ref[c, pl.ds(r0 + first_tap + kk, CONV_ROWS), :]
        conv_ref[c, pl.ds(r0, CONV_ROWS), :] = acc
        return carry

    lax.fori_loop(0, n_slabs * chunks, conv_body, 0, unroll=4)

    tot = jnp.zeros((bm, 128), jnp.float32)
    for c in range(n_slabs):
        sl = slice(c * 128, (c + 1) * 128)
        tot = tot + (conv_ref[c] + cb_ref[:, sl])
    mu = jnp.sum(tot, axis=-1, keepdims=True) / (n_slabs * 128)
    sq = jnp.zeros((bm, 128), jnp.float32)
    for c in range(n_slabs):
        sl = slice(c * 128, (c + 1) * 128)
        dv = conv_ref[c] + cb_ref[:, sl] - mu
        sq = sq + dv * dv
    var = jnp.sum(sq, axis=-1, keepdims=True) / (n_slabs * 128)
    inv = lax.rsqrt(var + EPS)
    acc = x_ref[...] + bo_ref[...]
    for c in range(n_slabs):
        sl = slice(c * 128, (c + 1) * 128)
        uf = (conv_ref[c] + cb_ref[:, sl] - mu) * inv * lg_ref[:, sl] + lb_ref[:, sl]
        y_ref[:, sl] = (jax.nn.silu(uf) * sz_ref[:, sl].astype(jnp.float32)).astype(y_ref.dtype)
        if c % K_CHUNK_HEADS == K_CHUNK_HEADS - 1:
            ks = slice((c + 1 - K_CHUNK_HEADS) * 128, (c + 1) * 128)
            acc = acc + jnp.dot(y_ref[:, ks], w_ref[ks, :], preferred_element_type=jnp.float32)
    out_ref[...] = acc


def _b_out(u, sz, x2, conv_w_slabs, conv_b, ln_g, ln_b, w_out, b_out, seq, bm=256):
    m, ch = u.shape
    n_slabs = ch // 128
    halo_blocks = bm // CONV_HALO
    row = lambda i: (i, 0)
    const2 = lambda i: (0, 0)
    return pl.pallas_call(
        functools.partial(_b_out_kernel, blocks_per_seq=seq // bm),
        grid=(m // bm,),
        in_specs=[
            pl.BlockSpec((bm, ch), row),
            pl.BlockSpec((CONV_HALO, ch), lambda i: (jnp.maximum(i * halo_blocks - 1, 0), 0)),
            pl.BlockSpec((bm, ch), row),
            pl.BlockSpec((bm, D_MODEL), row),
            pl.BlockSpec((n_slabs, CONV_HALO, 128), lambda i: (0, 0, 0)),
            pl.BlockSpec((1, ch), const2), pl.BlockSpec((1, ch), const2),
            pl.BlockSpec((1, ch), const2),
            pl.BlockSpec((ch, D_MODEL), const2),
            pl.BlockSpec((1, D_MODEL), const2),
        ],
        out_specs=pl.BlockSpec((bm, D_MODEL), row),
        out_shape=jax.ShapeDtypeStruct((m, D_MODEL), jnp.float32),
        scratch_shapes=[
            pltpu.VMEM((n_slabs, CONV_HALO + bm, 128), jnp.float32),
            pltpu.VMEM((n_slabs, bm, 128), jnp.float32),
            pltpu.VMEM((bm, ch), jnp.bfloat16),
        ],
        compiler_params=_params(1),
        name="b_conv_ln_out_proj",
    )(u, u, sz, x2, conv_w_slabs, conv_b, ln_g, ln_b, w_out, b_out)


def kernel(x, norm_g, rel_bias, a_w_in, a_q_gain, a_k_gain, a_w_out, b_w_in, b_b_in,
           b_conv_w, b_conv_b, b_ln_g, b_ln_b, b_w_out, b_b_out):
    batch, seq, dm = x.shape
    m = batch * seq
    bf16 = jnp.bfloat16
    x2 = x.reshape(m, dm)

    ones = jnp.ones((ATT_WIDTH,), jnp.float32)
    cols = []
    for g in range(N_GROUPS):
        cols += [jnp.tile(a_q_gain[0, g], N_HEADS) * (HEAD_DIM ** -0.5 * LOG2E), ones, ones]
    cols.append(ones)
    colscale = jnp.concatenate(cols).reshape(1, A_COLS)
    k_gain_tile = jnp.broadcast_to(a_k_gain[0][:, :, None], (N_GROUPS, HEAD_DIM, 128))

    w_in = a_w_in[0]
    hs = _a_norm(x2, norm_g[0:1], batch, seq)
    bias = _expand_bias(rel_bias, _bucket_tiles())
    os, lses, p0 = [], [], None
    for g in range(N_GROUPS):
        hg = hs[g].reshape(m, dm)
        qv = _a_in(hg, w_in, colscale, g, with_gate=(g == 0))
        kt = _a_in_k(hg, w_in, k_gain_tile, g)
        if g == 0:
            p0 = qv
        o, lse = _attention(qv, kt, bias, g, batch, seq)
        os.append(o)
        lses.append(lse)
    x2 = _a_out(os, lses, p0, x2, a_w_out[0].astype(bf16), batch, seq)

    u, sz = _b_in(x2, norm_g[1:2], b_w_in[0].astype(bf16), b_b_in[0:1])
    ch = u.shape[1]
    cw = jnp.pad(b_conv_w[0], ((0, CONV_HALO - CONV_WIDTH), (0, 0)))
    cw = cw.reshape(CONV_HALO, ch // 128, 128).transpose(1, 0, 2)
    x2 = _b_out(u, sz, x2, cw, b_conv_b[0:1], b_ln_g[0:1], b_ln_b[0:1],
                b_w_out[0].astype(bf16), b_b_out[0:1], seq)
    return x2.reshape(batch, seq, dm)
```

```python
import functools
import math

import jax
import jax.numpy as jnp
from jax import lax
from jax.experimental import pallas as pl
from jax.experimental.pallas import tpu as pltpu

D_MODEL = 2048
HEAD_DIM = 128
N_HEADS = 16
ATT_GROUPS = ((128, 1), (512, 4), (2048, 16))
N_GROUPS = 3
ATT_WIDTH = N_HEADS * HEAD_DIM
A_COLS = 3 * N_GROUPS * ATT_WIDTH + ATT_WIDTH
BLOCK = 128
N_BUCKETS = 32
MAX_DISTANCE = 2048
CONV_WIDTH = 31
EPS = 1e-6
NEG = -1e30
LOG2E = math.log2(math.e)
LN2 = math.log(2.0)

VMEM_LIMIT_BYTES = 56 * 1024 * 1024
CONV_HALO = 32
CONV_ROWS = 64
ATT_QB = 8
K_CHUNK_HEADS = 2


def _params(n_axes):
    return pltpu.CompilerParams(
        dimension_semantics=("arbitrary",) * n_axes,
        vmem_limit_bytes=VMEM_LIMIT_BYTES)


def _rmsnorm_to(h_ref, x_ref, g_ref, rows):
    n_chunks = x_ref.shape[0] // rows

    def body(c, carry):
        r0 = pl.multiple_of(c * rows, rows)
        xc = x_ref[pl.ds(r0, rows), :]
        ms = jnp.mean(xc * xc, axis=-1, keepdims=True)
        h_ref[pl.ds(r0, rows), :] = (xc * lax.rsqrt(ms + EPS) * g_ref[...]).astype(h_ref.dtype)
        return carry

    lax.fori_loop(0, n_chunks, body, 0)


def _a_norm_kernel(x_ref, g_ref, hn_ref, h4_ref, h16_ref, slab_ref):
    bm = x_ref.shape[0]
    n_slabs = x_ref.shape[1] // 128
    rows = 128
    for c0 in range(bm // rows):
        rs = slice(c0 * rows, (c0 + 1) * rows)
        xc = x_ref[rs, :]
        ms = jnp.mean(xc * xc, axis=-1, keepdims=True)
        hn = xc * lax.rsqrt(ms + EPS) * g_ref[...]
        hn_ref[rs, :] = hn.astype(hn_ref.dtype)
        for c in range(n_slabs):
            slab_ref[c, rs, :] = hn[:, c * 128:(c + 1) * 128]
    for c in range(n_slabs):
        sl = slice(c * 128, (c + 1) * 128)
        for r in range(4):
            h4_ref[0, r, :, sl] = slab_ref[c, pl.ds(r, bm // 4, stride=4), :].astype(h4_ref.dtype)
        for r in range(16):
            h16_ref[0, r, :, sl] = slab_ref[c, pl.ds(r, bm // 16, stride=16), :].astype(h16_ref.dtype)


def _a_norm(x2, g, batch, seq, bm=512):
    m, k = x2.shape
    per_seq = seq // bm
    bf16 = jnp.bfloat16

    def perm_spec(d):
        return pl.BlockSpec((1, d, bm // d, k), lambda i: (i // per_seq, 0, i % per_seq, 0))

    return pl.pallas_call(
        _a_norm_kernel,
        grid=(m // bm,),
        in_specs=[pl.BlockSpec((bm, k), lambda i: (i, 0)), pl.BlockSpec((1, k), lambda i: (0, 0))],
        out_specs=[pl.BlockSpec((bm, k), lambda i: (i, 0)), perm_spec(4), perm_spec(16)],
        out_shape=[
            jax.ShapeDtypeStruct((m, k), bf16),
            jax.ShapeDtypeStruct((batch, 4, seq // 4, k), bf16),
            jax.ShapeDtypeStruct((batch, 16, seq // 16, k), bf16),
        ],
        scratch_shapes=[pltpu.VMEM((k // 128, bm, 128), jnp.float32)],
        compiler_params=_params(1),
        name="a_rmsnorm_permute",
    )(x2, g)


def _a_in_kernel(h_ref, w_ref, cs_ref, o_ref, wb_ref, *, blocks_per_section):
    @pl.when(pl.program_id(1) == 0)
    def _():
        rows = 256
        for c0 in range(w_ref.shape[0] // rows):
            rs = slice(c0 * rows, (c0 + 1) * rows)
            wb_ref[rs, :] = w_ref[rs, :].astype(wb_ref.dtype)

    acc = jnp.dot(h_ref[...], wb_ref[...], preferred_element_type=jnp.float32)
    is_q = pl.program_id(0) < blocks_per_section
    for hh in range(acc.shape[1] // HEAD_DIM):
        sl = slice(hh * HEAD_DIM, (hh + 1) * HEAD_DIM)
        a = acc[:, sl]
        ms = jnp.mean(a * a, axis=-1, keepdims=True)
        scale = jnp.where(is_q, lax.rsqrt(ms + EPS) * cs_ref[:, sl], 1.0)
        o_ref[:, sl] = (a * scale).astype(o_ref.dtype)


def _a_in(h, w, colscale, g, with_gate, bm=1024, bn=2048):
    m, k = h.shape
    per_section = ATT_WIDTH // bn
    n_blocks = (3 if with_gate else 2) * per_section
    group0 = 3 * g * per_section
    gate0 = 3 * N_GROUPS * per_section

    def col(j, i):
        sec, off = j // per_section, j % per_section
        return (0, jnp.where(sec == 2, gate0 + off, group0 + 2 * sec * per_section + off))

    return pl.pallas_call(
        functools.partial(_a_in_kernel, blocks_per_section=per_section),
        grid=(n_blocks, m // bm),
        in_specs=[
            pl.BlockSpec((bm, k), lambda j, i: (i, 0)),
            pl.BlockSpec((k, bn), col, pipeline_mode=pl.Buffered(1)),
            pl.BlockSpec((1, bn), col),
        ],
        out_specs=pl.BlockSpec((bm, bn), lambda j, i: (i, j)),
        out_shape=jax.ShapeDtypeStruct((m, n_blocks * bn), jnp.bfloat16),
        scratch_shapes=[pltpu.VMEM((k, bn), jnp.bfloat16)],
        compiler_params=_params(2),
        name=f"a_in_proj_qv_g{g}",
    )(h, w, colscale)


def _a_in_k_kernel(h_ref, w_ref, gain_ref, o_ref, wt_ref):
    @pl.when(pl.program_id(1) == 0)
    def _():
        rows = 256
        for c0 in range(w_ref.shape[0] // rows):
            rs = slice(c0 * rows, (c0 + 1) * rows)
            wt_ref[:, rs] = w_ref[rs, :].T.astype(wt_ref.dtype)

    acc = lax.dot_general(wt_ref[...], h_ref[...], (((1,), (1,)), ((), ())),
                          preferred_element_type=jnp.float32)
    bm = acc.shape[1]
    for hh in range(acc.shape[0] // HEAD_DIM):
        hs = slice(hh * HEAD_DIM, (hh + 1) * HEAD_DIM)
        a = acc[hs, :]
        ms = jnp.mean(a * a, axis=0, keepdims=True)
        an = a * lax.rsqrt(ms + EPS)
        for lb in range(bm // 128):
            ls = slice(lb * 128, (lb + 1) * 128)
            o_ref[hs, ls] = (an[:, ls] * gain_ref[0]).astype(o_ref.dtype)


def _a_in_k(h, w, gain_tile, g, bm=1024, bn=1024):
    m, k = h.shape
    per_section = ATT_WIDTH // bn
    k0 = (3 * g + 1) * per_section
    return pl.pallas_call(
        _a_in_k_kernel,
        grid=(per_section, m // bm),
        in_specs=[
            pl.BlockSpec((bm, k), lambda j, i: (i, 0)),
            pl.BlockSpec((k, bn), lambda j, i: (0, k0 + j)),
            pl.BlockSpec((1, HEAD_DIM, 128), lambda j, i: (g, 0, 0)),
        ],
        out_specs=pl.BlockSpec((bn, bm), lambda j, i: (j, i)),
        out_shape=jax.ShapeDtypeStruct((ATT_WIDTH, m), jnp.bfloat16),
        scratch_shapes=[pltpu.VMEM((bn, k), jnp.bfloat16)],
        compiler_params=_params(2),
        name=f"a_in_proj_kT_g{g}",
    )(h, w, gain_tile)


def _bias_kernel(table_ref, bucket_ref, o_ref):
    g = pl.program_id(0)
    hh = pl.program_id(1)
    bucket = bucket_ref[0]
    acc = jnp.full(bucket.shape, NEG, jnp.float32)
    for b in range(N_BUCKETS):
        acc = jnp.where(bucket == b, table_ref[b, g * N_HEADS + hh] * LOG2E, acc)
    col = lax.broadcasted_iota(jnp.int32, bucket.shape, 1)
    o_ref[0, 0, 0] = acc
    o_ref[0, 1, 0] = jnp.where(col < BLOCK, NEG, acc)


def _expand_bias(rel_bias, bucket):
    return pl.pallas_call(
        _bias_kernel,
        grid=(N_GROUPS, N_HEADS),
        in_specs=[
            pl.BlockSpec(memory_space=pltpu.SMEM),
            pl.BlockSpec((1, BLOCK, 2 * BLOCK), lambda g, h: (g, 0, 0)),
        ],
        out_specs=pl.BlockSpec((1, 2, 1, BLOCK, 2 * BLOCK), lambda g, h: (g, 0, h, 0, 0)),
        out_shape=jax.ShapeDtypeStruct((N_GROUPS, 2, N_HEADS, BLOCK, 2 * BLOCK), jnp.float32),
        compiler_params=_params(2),
        name="rel_bias_expand",
    )(rel_bias, bucket)


def _bucket_tiles():
    max_exact = N_BUCKETS // 2
    qi = jnp.arange(BLOCK)[:, None] + BLOCK
    kj = jnp.arange(2 * BLOCK)[None, :]
    step = qi - kj
    tiles = []
    for window, dilation in ATT_GROUPS:
        steps = window // dilation
        in_window = (step >= 0) & (step <= steps)
        dist = jnp.maximum(step, 0) * dilation
        is_small = dist < max_exact
        ratio = jnp.log(jnp.maximum(dist, 1).astype(jnp.float32) / max_exact) / math.log(MAX_DISTANCE / max_exact)
        large = max_exact + (ratio * (N_BUCKETS - max_exact)).astype(jnp.int32)
        large = jnp.minimum(large, N_BUCKETS - 1)
        bucket = jnp.where(is_small, dist, large)
        tiles.append(jnp.where(in_window, bucket, -1).astype(jnp.int32))
    return jnp.stack(tiles, axis=0)


def _attn_kernel(q_ref, kt_ref, ktp_ref, v_ref, vp_ref, bias_ref, o_ref, lse_ref, kk_ref, vv_ref):
    n = pl.program_id(1)
    span = q_ref.shape[1]
    kk_ref[:, 0:BLOCK] = ktp_ref[...]
    kk_ref[:, BLOCK:BLOCK + span] = kt_ref[...]
    vv_ref[0:BLOCK, :] = vp_ref[0]
    vv_ref[BLOCK:BLOCK + span, :] = v_ref[0]

    lane = lax.broadcasted_iota(jnp.int32, (BLOCK, BLOCK), 1)
    for sb in range(span // BLOCK):
        rows = slice(sb * BLOCK, (sb + 1) * BLOCK)
        keys = slice(sb * BLOCK, (sb + 2) * BLOCK)
        variant = jnp.where(n == 0, 1, 0) if sb == 0 else 0
        lse_tile = jnp.zeros((BLOCK, BLOCK), jnp.float32)
        for hh in range(N_HEADS):
            sl = slice(hh * HEAD_DIM, (hh + 1) * HEAD_DIM)
            s = jnp.dot(q_ref[0, rows, sl], kk_ref[sl, keys], preferred_element_type=jnp.float32)
            s = s + bias_ref[0, variant, hh]
            m = jnp.max(s, axis=-1, keepdims=True)
            p = jnp.exp2(s - m)
            l = jnp.sum(p, axis=-1, keepdims=True)
            o = jnp.dot(p.astype(vv_ref.dtype), vv_ref[keys, sl], preferred_element_type=jnp.float32)
            o_ref[0, rows, sl] = (o / l).astype(o_ref.dtype)
            lse_tile = jnp.where(lane == hh, (m + jnp.log2(l)) * LN2, lse_tile)
        lse_ref[0, rows, :] = lse_tile


def _attention(qv, kt, bias, g, batch, seq):
    _, d = ATT_GROUPS[g]
    sub_len = seq // d
    qb = min(ATT_QB, sub_len // BLOCK)
    span = qb * BLOCK
    nb = sub_len // span
    qv3 = qv.reshape(batch * d, sub_len, qv.shape[1])

    return pl.pallas_call(
        _attn_kernel,
        grid=(batch * d, nb),
        in_specs=[
            pl.BlockSpec((1, span, ATT_WIDTH), lambda s, n: (s, n, 0)),
            pl.BlockSpec((ATT_WIDTH, span), lambda s, n: (0, s * nb + n)),
            pl.BlockSpec((ATT_WIDTH, BLOCK),
                         lambda s, n: (0, jnp.maximum((s * nb + n) * qb - 1, 0))),
            pl.BlockSpec((1, span, ATT_WIDTH), lambda s, n: (s, n, 1)),
            pl.BlockSpec((1, BLOCK, ATT_WIDTH), lambda s, n: (s, jnp.maximum(n * qb - 1, 0), 1)),
            pl.BlockSpec((1, 2, N_HEADS, BLOCK, 2 * BLOCK), lambda s, n: (g, 0, 0, 0, 0)),
        ],
        out_specs=[
            pl.BlockSpec((1, span, ATT_WIDTH), lambda s, n: (s, n, 0)),
            pl.BlockSpec((1, span, BLOCK), lambda s, n: (s, n, 0)),
        ],
        out_shape=[
            jax.ShapeDtypeStruct((batch * d, sub_len, ATT_WIDTH), jnp.bfloat16),
            jax.ShapeDtypeStruct((batch * d, sub_len, BLOCK), jnp.float32),
        ],
        scratch_shapes=[
            pltpu.VMEM((ATT_WIDTH, BLOCK + span), jnp.bfloat16),
            pltpu.VMEM((BLOCK + span, ATT_WIDTH), jnp.bfloat16),
        ],
        compiler_params=_params(2),
        name=f"dilated_attn_g{g}",
    )(qv3, kt, kt, qv3, qv3, bias)


def _a_out_kernel(o0_ref, o1_ref, o2_ref, l0_ref, l1_ref, l2_ref, z_ref, x_ref, w_ref,
                  out_ref, l1n_ref, l2n_ref, t1_ref, t2_ref, y_ref):
    bm = x_ref.shape[0]
    for r in range(4):
        l1n_ref[pl.ds(r, bm // 4, stride=4), :] = l1_ref[0, r]
    for r in range(16):
        l2n_ref[pl.ds(r, bm // 16, stride=16), :] = l2_ref[0, r]
    l0, l1, l2 = l0_ref[...], l1n_ref[...], l2n_ref[...]
    mx = jnp.maximum(jnp.maximum(l0, l1), l2)
    e0, e1, e2 = jnp.exp(l0 - mx), jnp.exp(l1 - mx), jnp.exp(l2 - mx)
    den = e0 + e1 + e2
    w0, w1, w2 = e0 / den, e1 / den, e2 / den
    acc = x_ref[...]
    for hh in range(N_HEADS):
        sl = slice(hh * HEAD_DIM, (hh + 1) * HEAD_DIM)
        for r in range(4):
            t1_ref[hh, pl.ds(r, bm // 4, stride=4), :] = o1_ref[0, r, :, sl].astype(jnp.float32)
        for r in range(16):
            t2_ref[hh, pl.ds(r, bm // 16, stride=16), :] = o2_ref[0, r, :, sl].astype(jnp.float32)
        o = (w0[:, hh:hh + 1] * o0_ref[:, sl].astype(jnp.float32)
             + w1[:, hh:hh + 1] * t1_ref[hh]
             + w2[:, hh:hh + 1] * t2_ref[hh])
        y_ref[:, sl] = (o * jax.nn.silu(z_ref[:, sl].astype(jnp.float32))).astype(y_ref.dtype)
        if hh % K_CHUNK_HEADS == K_CHUNK_HEADS - 1:
            ks = slice((hh + 1 - K_CHUNK_HEADS) * HEAD_DIM, (hh + 1) * HEAD_DIM)
            acc = acc + jnp.dot(y_ref[:, ks], w_ref[ks, :], preferred_element_type=jnp.float32)
    out_ref[...] = acc


def _a_out(os, lses, p0, x2, w_out, batch, seq, bm=256):
    m = x2.shape[0]
    per_seq = seq // bm
    z_block = 2
    row = lambda i: (i, 0)

    def perm_spec(d, width):
        return pl.BlockSpec((1, d, bm // d, width), lambda i: (i // per_seq, 0, i % per_seq, 0))

    def perm_view(a, d):
        return a.reshape(batch, d, seq // d, a.shape[-1])

    return pl.pallas_call(
        _a_out_kernel,
        grid=(m // bm,),
        in_specs=[
            pl.BlockSpec((bm, ATT_WIDTH), row), perm_spec(4, ATT_WIDTH), perm_spec(16, ATT_WIDTH),
            pl.BlockSpec((bm, BLOCK), row), perm_spec(4, BLOCK), perm_spec(16, BLOCK),
            pl.BlockSpec((bm, ATT_WIDTH), lambda i: (i, z_block)),
            pl.BlockSpec((bm, D_MODEL), row),
            pl.BlockSpec((ATT_WIDTH, D_MODEL), lambda i: (0, 0)),
        ],
        out_specs=pl.BlockSpec((bm, D_MODEL), row),
        out_shape=jax.ShapeDtypeStruct((m, D_MODEL), jnp.float32),
        scratch_shapes=[
            pltpu.VMEM((bm, BLOCK), jnp.float32),
            pltpu.VMEM((bm, BLOCK), jnp.float32),
            pltpu.VMEM((N_HEADS, bm, HEAD_DIM), jnp.float32),
            pltpu.VMEM((N_HEADS, bm, HEAD_DIM), jnp.float32),
            pltpu.VMEM((bm, ATT_WIDTH), jnp.bfloat16),
        ],
        compiler_params=_params(1),
        name="a_merge_out_proj",
    )(os[0].reshape(m, ATT_WIDTH), perm_view(os[1], 4), perm_view(os[2], 16),
      lses[0].reshape(m, BLOCK), perm_view(lses[1], 4), perm_view(lses[2], 16),
      p0, x2, w_out)


def _b_in_kernel(x_ref, g_ref, w_ref, b_ref, u_ref, sz_ref, h_ref, *, bn):
    bm, ch = u_ref.shape
    rows = 128
    for r0 in range(0, bm, rows):
        rs = slice(r0, r0 + rows)
        xc = x_ref[rs, :]
        ms = jnp.mean(xc * xc, axis=-1, keepdims=True)
        h_ref[rs, :] = (xc * lax.rsqrt(ms + EPS) * g_ref[...]).astype(h_ref.dtype)
    h = h_ref[...]
    for c0 in range(0, ch, bn):
        ca, cg, cz = (slice(s * ch + c0, s * ch + c0 + bn) for s in range(3))
        cs = slice(c0, c0 + bn)
        a = jnp.dot(h, w_ref[:, ca], preferred_element_type=jnp.float32) + b_ref[:, ca]
        ga = jnp.dot(h, w_ref[:, cg], preferred_element_type=jnp.float32) + b_ref[:, cg]
        u_ref[:, cs] = a * jax.nn.sigmoid(ga)
        z = jnp.dot(h, w_ref[:, cz], preferred_element_type=jnp.float32) + b_ref[:, cz]
        sz_ref[:, cs] = jax.nn.silu(z).astype(sz_ref.dtype)


def _b_in(x2, g, w, b, bm=256, bn=512):
    m, k = x2.shape
    ch = w.shape[1] // 3
    row = lambda i: (i, 0)
    const2 = lambda i: (0, 0)
    return pl.pallas_call(
        functools.partial(_b_in_kernel, bn=bn),
        grid=(m // bm,),
        in_specs=[
            pl.BlockSpec((bm, k), row),
            pl.BlockSpec((1, k), const2),
            pl.BlockSpec(w.shape, const2, pipeline_mode=pl.Buffered(1)),
            pl.BlockSpec(b.shape, const2),
        ],
        out_specs=[pl.BlockSpec((bm, ch), row), pl.BlockSpec((bm, ch), row)],
        out_shape=[
            jax.ShapeDtypeStruct((m, ch), jnp.float32),
            jax.ShapeDtypeStruct((m, ch), jnp.bfloat16),
        ],
        scratch_shapes=[pltpu.VMEM((bm, k), jnp.bfloat16)],
        compiler_params=_params(1),
        name="b_in_proj_glu",
    )(x2, g, w, b)


def _b_out_kernel(u_ref, halo_ref, sz_ref, x_ref, cw_ref, cb_ref, lg_ref, lb_ref,
                  w_ref, bo_ref, out_ref, ext_ref, conv_ref, y_ref, *, blocks_per_seq):
    bm = u_ref.shape[0]
    n_slabs = u_ref.shape[1] // 128
    i = pl.program_id(0)
    seq_start = (i % blocks_per_seq) == 0

    for c in range(n_slabs):
        sl = slice(c * 128, (c + 1) * 128)
        ext_ref[c, CONV_HALO:CONV_HALO + bm, :] = u_ref[:, sl]

    @pl.when(seq_start)
    def _():
        ext_ref[:, 0:CONV_HALO, :] = jnp.zeros((n_slabs, CONV_HALO, 128), jnp.float32)

    @pl.when(jnp.logical_not(seq_start))
    def _():
        for c in range(n_slabs):
            ext_ref[c, 0:CONV_HALO, :] = halo_ref[:, c * 128:(c + 1) * 128]

    first_tap = CONV_HALO - (CONV_WIDTH - 1)
    chunks = bm // CONV_ROWS

    def conv_body(t, carry):
        c = t // chunks
        r0 = pl.multiple_of((t % chunks) * CONV_ROWS, CONV_ROWS)
        acc = jnp.zeros((CONV_ROWS, 128), jnp.float32)
        for kk in range(CONV_WIDTH):
            acc = acc + cw_ref[c, kk:kk + 1, :] * ext_ref[c, pl.ds(r0 + first_tap + kk, CONV_ROWS), :]
        conv_ref[c, pl.ds(r0, CONV_ROWS), :] = acc
        return carry

    lax.fori_loop(0, n_slabs * chunks, conv_body, 0, unroll=4)

    tot = jnp.zeros((bm, 128), jnp.float32)
    for c in range(n_slabs):
        sl = slice(c * 128, (c + 1) * 128)
        tot = tot + (conv_ref[c] + cb_ref[:, sl])
    mu = jnp.sum(tot, axis=-1, keepdims=True) / (n_slabs * 128)
    sq = jnp.zeros((bm, 128), jnp.float32)
    for c in range(n_slabs):
        sl = slice(c * 128, (c + 1) * 128)
        dv = conv_ref[c] + cb_ref[:, sl] - mu
        sq = sq + dv * dv
    var = jnp.sum(sq, axis=-1, keepdims=True) / (n_slabs * 128)
    inv = lax.rsqrt(var + EPS)
    acc = x_ref[...] + bo_ref[...]
    for c in range(n_slabs):
        sl = slice(c * 128, (c + 1) * 128)
        uf = (conv_ref[c] + cb_ref[:, sl] - mu) * inv * lg_ref[:, sl] + lb_ref[:, sl]
        y_ref[:, sl] = (jax.nn.silu(uf) * sz_ref[:, sl].astype(jnp.float32)).astype(y_ref.dtype)
        if c % K_CHUNK_HEADS == K_CHUNK_HEADS - 1:
            ks = slice((c + 1 - K_CHUNK_HEADS) * 128, (c + 1) * 128)
            acc = acc + jnp.dot(y_ref[:, ks], w_ref[ks, :], preferred_element_type=jnp.float32)
    out_ref[...] = acc


def _b_out(u, sz, x2, conv_w_slabs, conv_b, ln_g, ln_b, w_out, b_out, seq, bm=256):
    m, ch = u.shape
    n_slabs = ch // 128
    halo_blocks = bm // CONV_HALO
    row = lambda i: (i, 0)
    const2 = lambda i: (0, 0)
    return pl.pallas_call(
        functools.partial(_b_out_kernel, blocks_per_seq=seq // bm),
        grid=(m // bm,),
        in_specs=[
            pl.BlockSpec((bm, ch), row),
            pl.BlockSpec((CONV_HALO, ch), lambda i: (jnp.maximum(i * halo_blocks - 1, 0), 0)),
            pl.BlockSpec((bm, ch), row),
            pl.BlockSpec((bm, D_MODEL), row),
            pl.BlockSpec((n_slabs, CONV_HALO, 128), lambda i: (0, 0, 0)),
            pl.BlockSpec((1, ch), const2), pl.BlockSpec((1, ch), const2),
            pl.BlockSpec((1, ch), const2),
            pl.BlockSpec((ch, D_MODEL), const2),
            pl.BlockSpec((1, D_MODEL), const2),
        ],
        out_specs=pl.BlockSpec((bm, D_MODEL), row),
        out_shape=jax.ShapeDtypeStruct((m, D_MODEL), jnp.float32),
        scratch_shapes=[
            pltpu.VMEM((n_slabs, CONV_HALO + bm, 128), jnp.float32),
            pltpu.VMEM((n_slabs, bm, 128), jnp.float32),
            pltpu.VMEM((bm, ch), jnp.bfloat16),
        ],
        compiler_params=_params(1),
        name="b_conv_ln_out_proj",
    )(u, u, sz, x2, conv_w_slabs, conv_b, ln_g, ln_b, w_out, b_out)


def kernel(x, norm_g, rel_bias, a_w_in, a_q_gain, a_k_gain, a_w_out, b_w_in, b_b_in,
           b_conv_w, b_conv_b, b_ln_g, b_ln_b, b_w_out, b_b_out):
    batch, seq, dm = x.shape
    m = batch * seq
    bf16 = jnp.bfloat16
    x2 = x.reshape(m, dm)

    ones = jnp.ones((ATT_WIDTH,), jnp.float32)
    cols = []
    for g in range(N_GROUPS):
        cols += [jnp.tile(a_q_gain[0, g], N_HEADS) * (HEAD_DIM ** -0.5 * LOG2E), ones, ones]
    cols.append(ones)
    colscale = jnp.concatenate(cols).reshape(1, A_COLS)
    k_gain_tile = jnp.broadcast_to(a_k_gain[0][:, :, None], (N_GROUPS, HEAD_DIM, 128))

    w_in = a_w_in[0]
    hs = _a_norm(x2, norm_g[0:1], batch, seq)
    bias = _expand_bias(rel_bias, _bucket_tiles())
    os, lses, p0 = [], [], None
    for g in range(N_GROUPS):
        hg = hs[g].reshape(m, dm)
        qv = _a_in(hg, w_in, colscale, g, with_gate=(g == 0))
        kt = _a_in_k(hg, w_in, k_gain_tile, g)
        if g == 0:
            p0 = qv
        o, lse = _attention(qv, kt, bias, g, batch, seq)
        os.append(o)
        lses.append(lse)
    x2 = _a_out(os, lses, p0, x2, a_w_out[0].astype(bf16), batch, seq)

    u, sz = _b_in(x2, norm_g[1:2], b_w_in[0].astype(bf16), b_b_in[0:1])
    ch = u.shape[1]
    cw = jnp.pad(b_conv_w[0], ((0, CONV_HALO - CONV_WIDTH), (0, 0)))
    cw = cw.reshape(CONV_HALO, ch // 128, 128).transpose(1, 0, 2)
    x2 = _b_out(u, sz, x2, cw, b_conv_b[0:1], b_ln_g[0:1], b_ln_b[0:1],
                b_w_out[0].astype(bf16), b_b_out[0:1], seq)
    return x2.reshape(batch, seq, dm)
```

```python
import functools
import math

import jax
import jax.numpy as jnp
from jax import lax
from jax.experimental import pallas as pl
from jax.experimental.pallas import tpu as pltpu

D_MODEL = 2048
HEAD_DIM = 128
N_HEADS = 16
ATT_GROUPS = ((128, 1), (512, 4), (2048, 16))
N_GROUPS = 3
ATT_WIDTH = N_HEADS * HEAD_DIM
A_COLS = 3 * N_GROUPS * ATT_WIDTH + ATT_WIDTH
BLOCK = 128
N_BUCKETS = 32
MAX_DISTANCE = 2048
CONV_WIDTH = 31
EPS = 1e-6
NEG = -1e30
LOG2E = math.log2(math.e)
LN2 = math.log(2.0)

VMEM_LIMIT_BYTES = 56 * 1024 * 1024
CONV_HALO = 32
CONV_ROWS = 64
ATT_QB = 8
K_CHUNK_HEADS = 2


def _params(n_axes):
    return pltpu.CompilerParams(
        dimension_semantics=("arbitrary",) * n_axes,
        vmem_limit_bytes=VMEM_LIMIT_BYTES)


def _rmsnorm_to(h_ref, x_ref, g_ref, rows):
    n_chunks = x_ref.shape[0] // rows

    def body(c, carry):
        r0 = pl.multiple_of(c * rows, rows)
        xc = x_ref[pl.ds(r0, rows), :]
        ms = jnp.mean(xc * xc, axis=-1, keepdims=True)
        h_ref[pl.ds(r0, rows), :] = (xc * lax.rsqrt(ms + EPS) * g_ref[...]).astype(h_ref.dtype)
        return carry

    lax.fori_loop(0, n_chunks, body, 0)


def _a_norm_kernel(x_ref, g_ref, hn_ref, h4_ref, h16_ref, slab_ref):
    bm = x_ref.shape[0]
    n_slabs = x_ref.shape[1] // 128
    rows = 128
    for c0 in range(bm // rows):
        rs = slice(c0 * rows, (c0 + 1) * rows)
        xc = x_ref[rs, :]
        ms = jnp.mean(xc * xc, axis=-1, keepdims=True)
        hn = xc * lax.rsqrt(ms + EPS) * g_ref[...]
        hn_ref[rs, :] = hn.astype(hn_ref.dtype)
        for c in range(n_slabs):
            slab_ref[c, rs, :] = hn[:, c * 128:(c + 1) * 128]
    for c in range(n_slabs):
        sl = slice(c * 128, (c + 1) * 128)
        for r in range(4):
            h4_ref[0, r, :, sl] = slab_ref[c, pl.ds(r, bm // 4, stride=4), :].astype(h4_ref.dtype)
        for r in range(16):
            h16_ref[0, r, :, sl] = slab_ref[c, pl.ds(r, bm // 16, stride=16), :].astype(h16_ref.dtype)


def _a_norm(x2, g, batch, seq, bm=512):
    m, k = x2.shape
    per_seq = seq // bm
    bf16 = jnp.bfloat16

    def perm_spec(d):
        return pl.BlockSpec((1, d, bm // d, k), lambda i: (i // per_seq, 0, i % per_seq, 0))

    return pl.pallas_call(
        _a_norm_kernel,
        grid=(m // bm,),
        in_specs=[pl.BlockSpec((bm, k), lambda i: (i, 0)), pl.BlockSpec((1, k), lambda i: (0, 0))],
        out_specs=[pl.BlockSpec((bm, k), lambda i: (i, 0)), perm_spec(4), perm_spec(16)],
        out_shape=[
            jax.ShapeDtypeStruct((m, k), bf16),
            jax.ShapeDtypeStruct((batch, 4, seq // 4, k), bf16),
            jax.ShapeDtypeStruct((batch, 16, seq // 16, k), bf16),
        ],
        scratch_shapes=[pltpu.VMEM((k // 128, bm, 128), jnp.float32)],
        compiler_params=_params(1),
        name="a_rmsnorm_permute",
    )(x2, g)


def _a_in_kernel(h_ref, w_ref, cs_ref, o_ref, wb_ref, *, blocks_per_section):
    @pl.when(pl.program_id(1) == 0)
    def _():
        rows = 256
        for c0 in range(w_ref.shape[0] // rows):
            rs = slice(c0 * rows, (c0 + 1) * rows)
            wb_ref[rs, :] = w_ref[rs, :].astype(wb_ref.dtype)

    acc = jnp.dot(h_ref[...], wb_ref[...], preferred_element_type=jnp.float32)
    is_q = pl.program_id(0) < blocks_per_section
    for hh in range(acc.shape[1] // HEAD_DIM):
        sl = slice(hh * HEAD_DIM, (hh + 1) * HEAD_DIM)
        a = acc[:, sl]
        ms = jnp.mean(a * a, axis=-1, keepdims=True)
        scale = jnp.where(is_q, lax.rsqrt(ms + EPS) * cs_ref[:, sl], 1.0)
        o_ref[:, sl] = (a * scale).astype(o_ref.dtype)


def _a_in(h, w, colscale, g, with_gate, bm=1024, bn=2048):
    m, k = h.shape
    per_section = ATT_WIDTH // bn
    n_blocks = (3 if with_gate else 2) * per_section
    group0 = 3 * g * per_section
    gate0 = 3 * N_GROUPS * per_section

    def col(j, i):
        sec, off = j // per_section, j % per_section
        return (0, jnp.where(sec == 2, gate0 + off, group0 + 2 * sec * per_section + off))

    return pl.pallas_call(
        functools.partial(_a_in_kernel, blocks_per_section=per_section),
        grid=(n_blocks, m // bm),
        in_specs=[
            pl.BlockSpec((bm, k), lambda j, i: (i, 0)),
            pl.BlockSpec((k, bn), col, pipeline_mode=pl.Buffered(1)),
            pl.BlockSpec((1, bn), col),
        ],
        out_specs=pl.BlockSpec((bm, bn), lambda j, i: (i, j)),
        out_shape=jax.ShapeDtypeStruct((m, n_blocks * bn), jnp.bfloat16),
        scratch_shapes=[pltpu.VMEM((k, bn), jnp.bfloat16)],
        compiler_params=_params(2),
        name=f"a_in_proj_qv_g{g}",
    )(h, w, colscale)


def _a_in_k_kernel(h_ref, w_ref, gain_ref, o_ref, wt_ref):
    @pl.when(pl.program_id(1) == 0)
    def _():
        rows = 256
        for c0 in range(w_ref.shape[0] // rows):
            rs = slice(c0 * rows, (c0 + 1) * rows)
            wt_ref[:, rs] = w_ref[rs, :].T.astype(wt_ref.dtype)

    acc = lax.dot_general(wt_ref[...], h_ref[...], (((1,), (1,)), ((), ())),
                          preferred_element_type=jnp.float32)
    bm = acc.shape[1]
    for hh in range(acc.shape[0] // HEAD_DIM):
        hs = slice(hh * HEAD_DIM, (hh + 1) * HEAD_DIM)
        a = acc[hs, :]
        ms = jnp.mean(a * a, axis=0, keepdims=True)
        an = a * lax.rsqrt(ms + EPS)
        for lb in range(bm // 128):
            ls = slice(lb * 128, (lb + 1) * 128)
            o_ref[hs, ls] = (an[:, ls] * gain_ref[0]).astype(o_ref.dtype)


def _a_in_k(h, w, gain_tile, g, bm=1024, bn=1024):
    m, k = h.shape
    per_section = ATT_WIDTH // bn
    k0 = (3 * g + 1) * per_section
    return pl.pallas_call(
        _a_in_k_kernel,
        grid=(per_section, m // bm),
        in_specs=[
            pl.BlockSpec((bm, k), lambda j, i: (i, 0)),
            pl.BlockSpec((k, bn), lambda j, i: (0, k0 + j)),
            pl.BlockSpec((1, HEAD_DIM, 128), lambda j, i: (g, 0, 0)),
        ],
        out_specs=pl.BlockSpec((bn, bm), lambda j, i: (j, i)),
        out_shape=jax.ShapeDtypeStruct((ATT_WIDTH, m), jnp.bfloat16),
        scratch_shapes=[pltpu.VMEM((bn, k), jnp.bfloat16)],
        compiler_params=_params(2),
        name=f"a_in_proj_kT_g{g}",
    )(h, w, gain_tile)


def _bias_kernel(table_ref, bucket_ref, o_ref):
    g = pl.program_id(0)
    rows = 8

    def body(c, carry):
        r0 = pl.multiple_of(c * rows, rows)
        bucket = bucket_ref[0, pl.ds(r0, rows), :]
        col = lax.broadcasted_iota(jnp.int32, bucket.shape, 1)
        accs = [jnp.full(bucket.shape, NEG, jnp.float32) for _ in range(N_HEADS)]
        for b in range(N_BUCKETS):
            hit = bucket == b
            for hh in range(N_HEADS):
                accs[hh] = jnp.where(hit, table_ref[b, g * N_HEADS + hh] * LOG2E, accs[hh])
        for hh in range(N_HEADS):
            o_ref[0, 0, hh, pl.ds(r0, rows), :] = accs[hh]
            o_ref[0, 1, hh, pl.ds(r0, rows), :] = jnp.where(col < BLOCK, NEG, accs[hh])
        return carry

    lax.fori_loop(0, BLOCK // rows, body, 0)


def _expand_bias(rel_bias, bucket):
    return pl.pallas_call(
        _bias_kernel,
        grid=(N_GROUPS,),
        in_specs=[
            pl.BlockSpec(memory_space=pltpu.SMEM),
            pl.BlockSpec((1, BLOCK, 2 * BLOCK), lambda g: (g, 0, 0)),
        ],
        out_specs=pl.BlockSpec((1, 2, N_HEADS, BLOCK, 2 * BLOCK), lambda g: (g, 0, 0, 0, 0)),
        out_shape=jax.ShapeDtypeStruct((N_GROUPS, 2, N_HEADS, BLOCK, 2 * BLOCK), jnp.float32),
        compiler_params=_params(1),
        name="rel_bias_expand",
    )(rel_bias, bucket)


def _bucket_tiles():
    max_exact = N_BUCKETS // 2
    qi = jnp.arange(BLOCK)[:, None] + BLOCK
    kj = jnp.arange(2 * BLOCK)[None, :]
    step = qi - kj
    tiles = []
    for window, dilation in ATT_GROUPS:
        steps = window // dilation
        in_window = (step >= 0) & (step <= steps)
        dist = jnp.maximum(step, 0) * dilation
        is_small = dist < max_exact
        ratio = jnp.log(jnp.maximum(dist, 1).astype(jnp.float32) / max_exact) / math.log(MAX_DISTANCE / max_exact)
        large = max_exact + (ratio * (N_BUCKETS - max_exact)).astype(jnp.int32)
        large = jnp.minimum(large, N_BUCKETS - 1)
        bucket = jnp.where(is_small, dist, large)
        tiles.append(jnp.where(in_window, bucket, -1).astype(jnp.int32))
    return jnp.stack(tiles, axis=0)


def _attn_kernel(q_ref, kt_ref, ktp_ref, v_ref, vp_ref, bias_ref, o_ref, lse_ref, kk_ref, vv_ref):
    n = pl.program_id(1)
    span = q_ref.shape[1]
    kk_ref[:, 0:BLOCK] = ktp_ref[...]
    kk_ref[:, BLOCK:BLOCK + span] = kt_ref[...]
    vv_ref[0:BLOCK, :] = vp_ref[0]
    vv_ref[BLOCK:BLOCK + span, :] = v_ref[0]

    lane = lax.broadcasted_iota(jnp.int32, (BLOCK, BLOCK), 1)
    for sb in range(span // BLOCK):
        rows = slice(sb * BLOCK, (sb + 1) * BLOCK)
        keys = slice(sb * BLOCK, (sb + 2) * BLOCK)
        variant = jnp.where(n == 0, 1, 0) if sb == 0 else 0
        lse_tile = jnp.zeros((BLOCK, BLOCK), jnp.float32)
        for hh in range(N_HEADS):
            sl = slice(hh * HEAD_DIM, (hh + 1) * HEAD_DIM)
            s = jnp.dot(q_ref[0, rows, sl], kk_ref[sl, keys], preferred_element_type=jnp.float32)
            s = s + bias_ref[0, variant, hh]
            m = jnp.max(s, axis=-1, keepdims=True)
            p = jnp.exp2(s - m)
            l = jnp.sum(p, axis=-1, keepdims=True)
            o = jnp.dot(p.astype(vv_ref.dtype), vv_ref[keys, sl], preferred_element_type=jnp.float32)
            o_ref[0, rows, sl] = (o / l).astype(o_ref.dtype)
            lse_tile = jnp.where(lane == hh, (m + jnp.log2(l)) * LN2, lse_tile)
        lse_ref[0, rows, :] = lse_tile


def _attention(qv, kt, bias, g, batch, seq):
    _, d = ATT_GROUPS[g]
    sub_len = seq // d
    qb = min(ATT_QB, sub_len // BLOCK)
    span = qb * BLOCK
    nb = sub_len // span
    qv3 = qv.reshape(batch * d, sub_len, qv.shape[1])

    return pl.pallas_call(
        _attn_kernel,
        grid=(batch * d, nb),
        in_specs=[
            pl.BlockSpec((1, span, ATT_WIDTH), lambda s, n: (s, n, 0)),
            pl.BlockSpec((ATT_WIDTH, span), lambda s, n: (0, s * nb + n)),
            pl.BlockSpec((ATT_WIDTH, BLOCK),
                         lambda s, n: (0, jnp.maximum((s * nb + n) * qb - 1, 0))),
            pl.BlockSpec((1, span, ATT_WIDTH), lambda s, n: (s, n, 1)),
            pl.BlockSpec((1, BLOCK, ATT_WIDTH), lambda s, n: (s, jnp.maximum(n * qb - 1, 0), 1)),
            pl.BlockSpec((1, 2, N_HEADS, BLOCK, 2 * BLOCK), lambda s, n: (g, 0, 0, 0, 0)),
        ],
        out_specs=[
            pl.BlockSpec((1, span, ATT_WIDTH), lambda s, n: (s, n, 0)),
            pl.BlockSpec((1, span, BLOCK), lambda s, n: (s, n, 0)),
        ],
        out_shape=[
            jax.ShapeDtypeStruct((batch * d, sub_len, ATT_WIDTH), jnp.bfloat16),
            jax.ShapeDtypeStruct((batch * d, sub_len, BLOCK), jnp.float32),
        ],
        scratch_shapes=[
            pltpu.VMEM((ATT_WIDTH, BLOCK + span), jnp.bfloat16),
            pltpu.VMEM((BLOCK + span, ATT_WIDTH), jnp.bfloat16),
        ],
        compiler_params=_params(2),
        name=f"dilated_attn_g{g}",
    )(qv3, kt, kt, qv3, qv3, bias)


def _cast_weight_once(wb_ref, w_ref, rows=256):
    @pl.when(pl.program_id(0) == 0)
    def _():
        for r0 in range(0, w_ref.shape[0], rows):
            wb_ref[r0:r0 + rows, :] = w_ref[r0:r0 + rows, :].astype(wb_ref.dtype)


def _a_out_kernel(o0_ref, o1_ref, o2_ref, l0_ref, l1_ref, l2_ref, z_ref, x_ref, w_ref,
                  out_ref, l1n_ref, l2n_ref, t1_ref, t2_ref, y_ref, wb_ref):
    bm = x_ref.shape[0]
    _cast_weight_once(wb_ref, w_ref)
    for r in range(4):
        l1n_ref[pl.ds(r, bm // 4, stride=4), :] = l1_ref[0, r]
    for r in range(16):
        l2n_ref[pl.ds(r, bm // 16, stride=16), :] = l2_ref[0, r]
    l0, l1, l2 = l0_ref[...], l1n_ref[...], l2n_ref[...]
    mx = jnp.maximum(jnp.maximum(l0, l1), l2)
    e0, e1, e2 = jnp.exp(l0 - mx), jnp.exp(l1 - mx), jnp.exp(l2 - mx)
    den = e0 + e1 + e2
    w0, w1, w2 = e0 / den, e1 / den, e2 / den
    acc = x_ref[...]
    for hh in range(N_HEADS):
        sl = slice(hh * HEAD_DIM, (hh + 1) * HEAD_DIM)
        for r in range(4):
            t1_ref[hh, pl.ds(r, bm // 4, stride=4), :] = o1_ref[0, r, :, sl].astype(jnp.float32)
        for r in range(16):
            t2_ref[hh, pl.ds(r, bm // 16, stride=16), :] = o2_ref[0, r, :, sl].astype(jnp.float32)
        o = (w0[:, hh:hh + 1] * o0_ref[:, sl].astype(jnp.float32)
             + w1[:, hh:hh + 1] * t1_ref[hh]
             + w2[:, hh:hh + 1] * t2_ref[hh])
        y_ref[:, sl] = (o * jax.nn.silu(z_ref[:, sl].astype(jnp.float32))).astype(y_ref.dtype)
        if hh % K_CHUNK_HEADS == K_CHUNK_HEADS - 1:
            ks = slice((hh + 1 - K_CHUNK_HEADS) * HEAD_DIM, (hh + 1) * HEAD_DIM)
            acc = acc + jnp.dot(y_ref[:, ks], wb_ref[ks, :], preferred_element_type=jnp.float32)
    out_ref[...] = acc


def _a_out(os, lses, p0, x2, w_out, batch, seq, bm=256):
    m = x2.shape[0]
    per_seq = seq // bm
    z_block = 2
    row = lambda i: (i, 0)

    def perm_spec(d, width):
        return pl.BlockSpec((1, d, bm // d, width), lambda i: (i // per_seq, 0, i % per_seq, 0))

    def perm_view(a, d):
        return a.reshape(batch, d, seq // d, a.shape[-1])

    return pl.pallas_call(
        _a_out_kernel,
        grid=(m // bm,),
        in_specs=[
            pl.BlockSpec((bm, ATT_WIDTH), row), perm_spec(4, ATT_WIDTH), perm_spec(16, ATT_WIDTH),
            pl.BlockSpec((bm, BLOCK), row), perm_spec(4, BLOCK), perm_spec(16, BLOCK),
            pl.BlockSpec((bm, ATT_WIDTH), lambda i: (i, z_block)),
            pl.BlockSpec((bm, D_MODEL), row),
            pl.BlockSpec((ATT_WIDTH, D_MODEL), lambda i: (0, 0), pipeline_mode=pl.Buffered(1)),
        ],
        out_specs=pl.BlockSpec((bm, D_MODEL), row),
        out_shape=jax.ShapeDtypeStruct((m, D_MODEL), jnp.float32),
        scratch_shapes=[
            pltpu.VMEM((bm, BLOCK), jnp.float32),
            pltpu.VMEM((bm, BLOCK), jnp.float32),
            pltpu.VMEM((N_HEADS, bm, HEAD_DIM), jnp.float32),
            pltpu.VMEM((N_HEADS, bm, HEAD_DIM), jnp.float32),
            pltpu.VMEM((bm, ATT_WIDTH), jnp.bfloat16),
            pltpu.VMEM((ATT_WIDTH, D_MODEL), jnp.bfloat16),
        ],
        compiler_params=_params(1),
        name="a_merge_out_proj",
    )(os[0].reshape(m, ATT_WIDTH), perm_view(os[1], 4), perm_view(os[2], 16),
      lses[0].reshape(m, BLOCK), perm_view(lses[1], 4), perm_view(lses[2], 16),
      p0, x2, w_out)


def _b_in_kernel(x_ref, g_ref, w_ref, b_ref, u_ref, sz_ref, h_ref, *, bn):
    bm, ch = u_ref.shape
    rows = 128
    for r0 in range(0, bm, rows):
        rs = slice(r0, r0 + rows)
        xc = x_ref[rs, :]
        ms = jnp.mean(xc * xc, axis=-1, keepdims=True)
        h_ref[rs, :] = (xc * lax.rsqrt(ms + EPS) * g_ref[...]).astype(h_ref.dtype)
    h = h_ref[...]
    for c0 in range(0, ch, bn):
        ca, cg, cz = (slice(s * ch + c0, s * ch + c0 + bn) for s in range(3))
        cs = slice(c0, c0 + bn)
        a = jnp.dot(h, w_ref[:, ca], preferred_element_type=jnp.float32) + b_ref[:, ca]
        ga = jnp.dot(h, w_ref[:, cg], preferred_element_type=jnp.float32) + b_ref[:, cg]
        u_ref[:, cs] = a * jax.nn.sigmoid(ga)
        z = jnp.dot(h, w_ref[:, cz], preferred_element_type=jnp.float32) + b_ref[:, cz]
        sz_ref[:, cs] = jax.nn.silu(z).astype(sz_ref.dtype)


def _b_in(x2, g, w, b, bm=256, bn=512):
    m, k = x2.shape
    ch = w.shape[1] // 3
    row = lambda i: (i, 0)
    const2 = lambda i: (0, 0)
    return pl.pallas_call(
        functools.partial(_b_in_kernel, bn=bn),
        grid=(m // bm,),
        in_specs=[
            pl.BlockSpec((bm, k), row),
            pl.BlockSpec((1, k), const2),
            pl.BlockSpec(w.shape, const2, pipeline_mode=pl.Buffered(1)),
            pl.BlockSpec(b.shape, const2),
        ],
        out_specs=[pl.BlockSpec((bm, ch), row), pl.BlockSpec((bm, ch), row)],
        out_shape=[
            jax.ShapeDtypeStruct((m, ch), jnp.float32),
            jax.ShapeDtypeStruct((m, ch), jnp.bfloat16),
        ],
        scratch_shapes=[pltpu.VMEM((bm, k), jnp.bfloat16)],
        compiler_params=_params(1),
        name="b_in_proj_glu",
    )(x2, g, w, b)


def _b_out_kernel(u_ref, halo_ref, sz_ref, x_ref, cw_ref, cb_ref, lg_ref, lb_ref,
                  w_ref, bo_ref, out_ref, ext_ref, conv_ref, y_ref, wb_ref, *, blocks_per_seq):
    bm = u_ref.shape[0]
    _cast_weight_once(wb_ref, w_ref)
    n_slabs = u_ref.shape[1] // 128
    i = pl.program_id(0)
    seq_start = (i % blocks_per_seq) == 0

    for c in range(n_slabs):
        sl = slice(c * 128, (c + 1) * 128)
        ext_ref[c, CONV_HALO:CONV_HALO + bm, :] = u_ref[:, sl]

    @pl.when(seq_start)
    def _():
        ext_ref[:, 0:CONV_HALO, :] = jnp.zeros((n_slabs, CONV_HALO, 128), jnp.float32)

    @pl.when(jnp.logical_not(seq_start))
    def _():
        for c in range(n_slabs):
            ext_ref[c, 0:CONV_HALO, :] = halo_ref[:, c * 128:(c + 1) * 128]

    first_tap = CONV_HALO - (CONV_WIDTH - 1)
    chunks = bm // CONV_ROWS

    def conv_body(t, carry):
        c = t // chunks
        r0 = pl.multiple_of((t % chunks) * CONV_ROWS, CONV_ROWS)
        acc = jnp.zeros((CONV_ROWS, 128), jnp.float32)
        for kk in range(CONV_WIDTH):
            acc = acc + cw_ref[c, kk:kk + 1, :] * ext_ref[c, pl.ds(r0 + first_tap + kk, CONV_ROWS), :]
        conv_ref[c, pl.ds(r0, CONV_ROWS), :] = acc
        return carry

    lax.fori_loop(0, n_slabs * chunks, conv_body, 0, unroll=4)

    tot = jnp.zeros((bm, 128), jnp.float32)
    for c in range(n_slabs):
        sl = slice(c * 128, (c + 1) * 128)
        tot = tot + (conv_ref[c] + cb_ref[:, sl])
    mu = jnp.sum(tot, axis=-1, keepdims=True) / (n_slabs * 128)
    sq = jnp.zeros((bm, 128), jnp.float32)
    for c in range(n_slabs):
        sl = slice(c * 128, (c + 1) * 128)
        dv = conv_ref[c] + cb_ref[:, sl] - mu
        sq = sq + dv * dv
    var = jnp.sum(sq, axis=-1, keepdims=True) / (n_slabs * 128)
    inv = lax.rsqrt(var + EPS)
    acc = x_ref[...] + bo_ref[...]
    for c in range(n_slabs):
        sl = slice(c * 128, (c + 1) * 128)
        uf = (conv_ref[c] + cb_ref[:, sl] - mu) * inv * lg_ref[:, sl] + lb_ref[:, sl]
        y_ref[:, sl] = (jax.nn.silu(uf) * sz_ref[:, sl].astype(jnp.float32)).astype(y_ref.dtype)
        if c % K_CHUNK_HEADS == K_CHUNK_HEADS - 1:
            ks = slice((c + 1 - K_CHUNK_HEADS) * 128, (c + 1) * 128)
            acc = acc + jnp.dot(y_ref[:, ks], wb_ref[ks, :], preferred_element_type=jnp.float32)
    out_ref[...] = acc


def _b_out(u, sz, x2, conv_w_slabs, conv_b, ln_g, ln_b, w_out, b_out, seq, bm=256):
    m, ch = u.shape
    n_slabs = ch // 128
    halo_blocks = bm // CONV_HALO
    row = lambda i: (i, 0)
    const2 = lambda i: (0, 0)
    return pl.pallas_call(
        functools.partial(_b_out_kernel, blocks_per_seq=seq // bm),
        grid=(m // bm,),
        in_specs=[
            pl.BlockSpec((bm, ch), row),
            pl.BlockSpec((CONV_HALO, ch), lambda i: (jnp.maximum(i * halo_blocks - 1, 0), 0)),
            pl.BlockSpec((bm, ch), row),
            pl.BlockSpec((bm, D_MODEL), row),
            pl.BlockSpec((n_slabs, CONV_HALO, 128), lambda i: (0, 0, 0)),
            pl.BlockSpec((1, ch), const2), pl.BlockSpec((1, ch), const2),
            pl.BlockSpec((1, ch), const2),
            pl.BlockSpec((ch, D_MODEL), const2, pipeline_mode=pl.Buffered(1)),
            pl.BlockSpec((1, D_MODEL), const2),
        ],
        out_specs=pl.BlockSpec((bm, D_MODEL), row),
        out_shape=jax.ShapeDtypeStruct((m, D_MODEL), jnp.float32),
        scratch_shapes=[
            pltpu.VMEM((n_slabs, CONV_HALO + bm, 128), jnp.float32),
            pltpu.VMEM((n_slabs, bm, 128), jnp.float32),
            pltpu.VMEM((bm, ch), jnp.bfloat16),
            pltpu.VMEM((ch, D_MODEL), jnp.bfloat16),
        ],
        compiler_params=_params(1),
        name="b_conv_ln_out_proj",
    )(u, u, sz, x2, conv_w_slabs, conv_b, ln_g, ln_b, w_out, b_out)


def kernel(x, norm_g, rel_bias, a_w_in, a_q_gain, a_k_gain, a_w_out, b_w_in, b_b_in,
           b_conv_w, b_conv_b, b_ln_g, b_ln_b, b_w_out, b_b_out):
    batch, seq, dm = x.shape
    m = batch * seq
    bf16 = jnp.bfloat16
    x2 = x.reshape(m, dm)

    ones = jnp.ones((ATT_WIDTH,), jnp.float32)
    cols = []
    for g in range(N_GROUPS):
        cols += [jnp.tile(a_q_gain[0, g], N_HEADS) * (HEAD_DIM ** -0.5 * LOG2E), ones, ones]
    cols.append(ones)
    colscale = jnp.concatenate(cols).reshape(1, A_COLS)
    k_gain_tile = jnp.broadcast_to(a_k_gain[0][:, :, None], (N_GROUPS, HEAD_DIM, 128))

    w_in = a_w_in[0]
    hs = _a_norm(x2, norm_g[0:1], batch, seq)
    bias = _expand_bias(rel_bias, _bucket_tiles())
    os, lses, p0 = [], [], None
    for g in range(N_GROUPS):
        hg = hs[g].reshape(m, dm)
        qv = _a_in(hg, w_in, colscale, g, with_gate=(g == 0))
        kt = _a_in_k(hg, w_in, k_gain_tile, g)
        if g == 0:
            p0 = qv
        o, lse = _attention(qv, kt, bias, g, batch, seq)
        os.append(o)
        lses.append(lse)
    x2 = _a_out(os, lses, p0, x2, a_w_out[0], batch, seq)

    u, sz = _b_in(x2, norm_g[1:2], b_w_in[0].astype(bf16), b_b_in[0:1])
    ch = u.shape[1]
    cw = jnp.pad(b_conv_w[0], ((0, CONV_HALO - CONV_WIDTH), (0, 0)))
    cw = cw.reshape(CONV_HALO, ch // 128, 128).transpose(1, 0, 2)
    x2 = _b_out(u, sz, x2, cw, b_conv_b[0:1], b_ln_g[0:1], b_ln_b[0:1],
                b_w_out[0], b_b_out[0:1], seq)
    return x2.reshape(batch, seq, dm)
```

```python
import functools
import math

import jax
import jax.numpy as jnp
from jax import lax
from jax.experimental import pallas as pl
from jax.experimental.pallas import tpu as pltpu

D_MODEL = 2048
HEAD_DIM = 128
N_HEADS = 16
ATT_GROUPS = ((128, 1), (512, 4), (2048, 16))
N_GROUPS = 3
ATT_WIDTH = N_HEADS * HEAD_DIM
A_COLS = 3 * N_GROUPS * ATT_WIDTH + ATT_WIDTH
BLOCK = 128
N_BUCKETS = 32
MAX_DISTANCE = 2048
CONV_WIDTH = 31
EPS = 1e-6
NEG = -1e30
LOG2E = math.log2(math.e)
LN2 = math.log(2.0)

VMEM_LIMIT_BYTES = 56 * 1024 * 1024
CONV_HALO = 32
CONV_ROWS = 64
ATT_QB = 8
K_CHUNK_HEADS = 2


def _params(n_axes):
    return pltpu.CompilerParams(
        dimension_semantics=("arbitrary",) * n_axes,
        vmem_limit_bytes=VMEM_LIMIT_BYTES)


def _rmsnorm_to(h_ref, x_ref, g_ref, rows):
    n_chunks = x_ref.shape[0] // rows

    def body(c, carry):
        r0 = pl.multiple_of(c * rows, rows)
        xc = x_ref[pl.ds(r0, rows), :]
        ms = jnp.mean(xc * xc, axis=-1, keepdims=True)
        h_ref[pl.ds(r0, rows), :] = (xc * lax.rsqrt(ms + EPS) * g_ref[...]).astype(h_ref.dtype)
        return carry

    lax.fori_loop(0, n_chunks, body, 0)


def _a_norm_kernel(x_ref, g_ref, hn_ref, h4_ref, h16_ref, slab_ref):
    bm = x_ref.shape[0]
    n_slabs = x_ref.shape[1] // 128
    rows = 128
    for c0 in range(bm // rows):
        rs = slice(c0 * rows, (c0 + 1) * rows)
        xc = x_ref[rs, :]
        ms = jnp.mean(xc * xc, axis=-1, keepdims=True)
        hn = xc * lax.rsqrt(ms + EPS) * g_ref[...]
        hn_ref[rs, :] = hn.astype(hn_ref.dtype)
        for c in range(n_slabs):
            slab_ref[c, rs, :] = hn[:, c * 128:(c + 1) * 128]
    for c in range(n_slabs):
        sl = slice(c * 128, (c + 1) * 128)
        for r in range(4):
            h4_ref[0, r, :, sl] = slab_ref[c, pl.ds(r, bm // 4, stride=4), :].astype(h4_ref.dtype)
        for r in range(16):
            h16_ref[0, r, :, sl] = slab_ref[c, pl.ds(r, bm // 16, stride=16), :].astype(h16_ref.dtype)


def _a_norm(x2, g, batch, seq, bm=512):
    m, k = x2.shape
    per_seq = seq // bm
    bf16 = jnp.bfloat16

    def perm_spec(d):
        return pl.BlockSpec((1, d, bm // d, k), lambda i: (i // per_seq, 0, i % per_seq, 0))

    return pl.pallas_call(
        _a_norm_kernel,
        grid=(m // bm,),
        in_specs=[pl.BlockSpec((bm, k), lambda i: (i, 0)), pl.BlockSpec((1, k), lambda i: (0, 0))],
        out_specs=[pl.BlockSpec((bm, k), lambda i: (i, 0)), perm_spec(4), perm_spec(16)],
        out_shape=[
            jax.ShapeDtypeStruct((m, k), bf16),
            jax.ShapeDtypeStruct((batch, 4, seq // 4, k), bf16),
            jax.ShapeDtypeStruct((batch, 16, seq // 16, k), bf16),
        ],
        scratch_shapes=[pltpu.VMEM((k // 128, bm, 128), jnp.float32)],
        compiler_params=_params(1),
        name="a_rmsnorm_permute",
    )(x2, g)


def _a_in_kernel(h_ref, wl_ref, wr_ref, cs_ref, o_ref, wb_ref, *, n_qv_steps):
    half = wl_ref.shape[1]

    @pl.when(pl.program_id(1) == 0)
    def _():
        rows = 256
        for r0 in range(0, wl_ref.shape[0], rows):
            wb_ref[r0:r0 + rows, 0:half] = wl_ref[r0:r0 + rows, :].astype(wb_ref.dtype)
            wb_ref[r0:r0 + rows, half:2 * half] = wr_ref[r0:r0 + rows, :].astype(wb_ref.dtype)

    acc = jnp.dot(h_ref[...], wb_ref[...], preferred_element_type=jnp.float32)
    is_q = pl.program_id(0) < n_qv_steps
    for hh in range(half // HEAD_DIM):
        sl = slice(hh * HEAD_DIM, (hh + 1) * HEAD_DIM)
        a = acc[:, sl]
        ms = jnp.mean(a * a, axis=-1, keepdims=True)
        scale = jnp.where(is_q, lax.rsqrt(ms + EPS) * cs_ref[:, sl], 1.0)
        o_ref[:, sl] = (a * scale).astype(o_ref.dtype)
    o_ref[:, half:2 * half] = acc[:, half:2 * half].astype(o_ref.dtype)


def _a_in(h, w, colscale, g, with_gate, bm=1024, half=1024):
    m, k = h.shape
    per_section = ATT_WIDTH // half
    n_steps = per_section + (1 if with_gate else 0) * (per_section // 2)
    q0 = 3 * g * per_section
    v0 = q0 + 2 * per_section
    gate0 = 3 * N_GROUPS * per_section

    def left(j, i):
        return (0, jnp.where(j < per_section, q0 + j, gate0 + 2 * (j - per_section)))

    def right(j, i):
        return (0, jnp.where(j < per_section, v0 + j, gate0 + 2 * (j - per_section) + 1))

    return pl.pallas_call(
        functools.partial(_a_in_kernel, n_qv_steps=per_section),
        grid=(n_steps, m // bm),
        in_specs=[
            pl.BlockSpec((bm, k), lambda j, i: (i, 0)),
            pl.BlockSpec((k, half), left, pipeline_mode=pl.Buffered(1)),
            pl.BlockSpec((k, half), right, pipeline_mode=pl.Buffered(1)),
            pl.BlockSpec((1, half), left),
        ],
        out_specs=pl.BlockSpec((bm, 2 * half), lambda j, i: (i, j)),
        out_shape=jax.ShapeDtypeStruct((m, n_steps * 2 * half), jnp.bfloat16),
        scratch_shapes=[pltpu.VMEM((k, 2 * half), jnp.bfloat16)],
        compiler_params=_params(2),
        name=f"a_in_proj_qv_g{g}",
    )(h, w, w, colscale)


def _a_in_k_kernel(h_ref, w_ref, gain_ref, o_ref, wt_ref):
    @pl.when(pl.program_id(1) == 0)
    def _():
        rows = 256
        for c0 in range(w_ref.shape[0] // rows):
            rs = slice(c0 * rows, (c0 + 1) * rows)
            wt_ref[:, rs] = w_ref[rs, :].T.astype(wt_ref.dtype)

    acc = lax.dot_general(wt_ref[...], h_ref[...], (((1,), (1,)), ((), ())),
                          preferred_element_type=jnp.float32)
    bm = acc.shape[1]
    for hh in range(acc.shape[0] // HEAD_DIM):
        hs = slice(hh * HEAD_DIM, (hh + 1) * HEAD_DIM)
        a = acc[hs, :]
        ms = jnp.mean(a * a, axis=0, keepdims=True)
        an = a * lax.rsqrt(ms + EPS)
        for lb in range(bm // 128):
            ls = slice(lb * 128, (lb + 1) * 128)
            o_ref[hs, ls] = (an[:, ls] * gain_ref[0]).astype(o_ref.dtype)


def _a_in_k(h, w, gain_tile, g, bm=1024, bn=1024):
    m, k = h.shape
    per_section = ATT_WIDTH // bn
    k0 = (3 * g + 1) * per_section
    return pl.pallas_call(
        _a_in_k_kernel,
        grid=(per_section, m // bm),
        in_specs=[
            pl.BlockSpec((bm, k), lambda j, i: (i, 0)),
            pl.BlockSpec((k, bn), lambda j, i: (0, k0 + j)),
            pl.BlockSpec((1, HEAD_DIM, 128), lambda j, i: (g, 0, 0)),
        ],
        out_specs=pl.BlockSpec((bn, bm), lambda j, i: (j, i)),
        out_shape=jax.ShapeDtypeStruct((ATT_WIDTH, m), jnp.bfloat16),
        scratch_shapes=[pltpu.VMEM((bn, k), jnp.bfloat16)],
        compiler_params=_params(2),
        name=f"a_in_proj_kT_g{g}",
    )(h, w, gain_tile)


def _bias_kernel(table_ref, bucket_ref, o_ref):
    g = pl.program_id(0)
    hh = pl.program_id(1)
    bucket = bucket_ref[0]
    acc = jnp.full(bucket.shape, NEG, jnp.float32)
    for b in range(N_BUCKETS):
        acc = jnp.where(bucket == b, table_ref[b, g * N_HEADS + hh] * LOG2E, acc)
    col = lax.broadcasted_iota(jnp.int32, bucket.shape, 1)
    o_ref[0, 0, 0] = acc
    o_ref[0, 1, 0] = jnp.where(col < BLOCK, NEG, acc)


def _expand_bias(rel_bias, bucket):
    return pl.pallas_call(
        _bias_kernel,
        grid=(N_GROUPS, N_HEADS),
        in_specs=[
            pl.BlockSpec(memory_space=pltpu.SMEM),
            pl.BlockSpec((1, BLOCK, 2 * BLOCK), lambda g, h: (g, 0, 0)),
        ],
        out_specs=pl.BlockSpec((1, 2, 1, BLOCK, 2 * BLOCK), lambda g, h: (g, 0, h, 0, 0)),
        out_shape=jax.ShapeDtypeStruct((N_GROUPS, 2, N_HEADS, BLOCK, 2 * BLOCK), jnp.float32),
        compiler_params=_params(2),
        name="rel_bias_expand",
    )(rel_bias, bucket)


def _bucket_tiles():
    max_exact = N_BUCKETS // 2
    qi = jnp.arange(BLOCK)[:, None] + BLOCK
    kj = jnp.arange(2 * BLOCK)[None, :]
    step = qi - kj
    tiles = []
    for window, dilation in ATT_GROUPS:
        steps = window // dilation
        in_window = (step >= 0) & (step <= steps)
        dist = jnp.maximum(step, 0) * dilation
        is_small = dist < max_exact
        ratio = jnp.log(jnp.maximum(dist, 1).astype(jnp.float32) / max_exact) / math.log(MAX_DISTANCE / max_exact)
        large = max_exact + (ratio * (N_BUCKETS - max_exact)).astype(jnp.int32)
        large = jnp.minimum(large, N_BUCKETS - 1)
        bucket = jnp.where(is_small, dist, large)
        tiles.append(jnp.where(in_window, bucket, -1).astype(jnp.int32))
    return jnp.stack(tiles, axis=0)


def _attn_kernel(qa_ref, qb_ref, kt_ref, ktp_ref, va_ref, vb_ref, vpa_ref, vpb_ref, bias_ref,
                 o_ref, lse_ref, kk_ref, vv_ref):
    n = pl.program_id(1)
    span = qa_ref.shape[1]
    half = qa_ref.shape[2]
    q_refs = (qa_ref, qb_ref)
    kk_ref[:, 0:BLOCK] = ktp_ref[...]
    kk_ref[:, BLOCK:BLOCK + span] = kt_ref[...]
    vv_ref[0:BLOCK, 0:half] = vpa_ref[0]
    vv_ref[0:BLOCK, half:2 * half] = vpb_ref[0]
    vv_ref[BLOCK:BLOCK + span, 0:half] = va_ref[0]
    vv_ref[BLOCK:BLOCK + span, half:2 * half] = vb_ref[0]

    lane = lax.broadcasted_iota(jnp.int32, (BLOCK, BLOCK), 1)
    for sb in range(span // BLOCK):
        rows = slice(sb * BLOCK, (sb + 1) * BLOCK)
        keys = slice(sb * BLOCK, (sb + 2) * BLOCK)
        variant = jnp.where(n == 0, 1, 0) if sb == 0 else 0
        lse_tile = jnp.zeros((BLOCK, BLOCK), jnp.float32)
        for hh in range(N_HEADS):
            sl = slice(hh * HEAD_DIM, (hh + 1) * HEAD_DIM)
            q_ref = q_refs[(hh * HEAD_DIM) // half]
            qs = slice((hh * HEAD_DIM) % half, (hh * HEAD_DIM) % half + HEAD_DIM)
            s = jnp.dot(q_ref[0, rows, qs], kk_ref[sl, keys], preferred_element_type=jnp.float32)
            s = s + bias_ref[0, variant, hh]
            m = jnp.max(s, axis=-1, keepdims=True)
            p = jnp.exp2(s - m)
            l = jnp.sum(p, axis=-1, keepdims=True)
            o = jnp.dot(p.astype(vv_ref.dtype), vv_ref[keys, sl], preferred_element_type=jnp.float32)
            o_ref[0, rows, sl] = (o / l).astype(o_ref.dtype)
            lse_tile = jnp.where(lane == hh, (m + jnp.log2(l)) * LN2, lse_tile)
        lse_ref[0, rows, :] = lse_tile


def _attention(qv, kt, bias, g, batch, seq):
    _, d = ATT_GROUPS[g]
    sub_len = seq // d
    qb = min(ATT_QB, sub_len // BLOCK)
    span = qb * BLOCK
    nb = sub_len // span
    qv3 = qv.reshape(batch * d, sub_len, qv.shape[1])
    half = ATT_WIDTH // 2

    def cur_spec(col):
        return pl.BlockSpec((1, span, half), lambda s, n: (s, n, col))

    def prev_spec(col):
        return pl.BlockSpec((1, BLOCK, half), lambda s, n: (s, jnp.maximum(n * qb - 1, 0), col))

    return pl.pallas_call(
        _attn_kernel,
        grid=(batch * d, nb),
        in_specs=[
            cur_spec(0), cur_spec(2),
            pl.BlockSpec((ATT_WIDTH, span), lambda s, n: (0, s * nb + n)),
            pl.BlockSpec((ATT_WIDTH, BLOCK),
                         lambda s, n: (0, jnp.maximum((s * nb + n) * qb - 1, 0))),
            cur_spec(1), cur_spec(3), prev_spec(1), prev_spec(3),
            pl.BlockSpec((1, 2, N_HEADS, BLOCK, 2 * BLOCK), lambda s, n: (g, 0, 0, 0, 0)),
        ],
        out_specs=[
            pl.BlockSpec((1, span, ATT_WIDTH), lambda s, n: (s, n, 0)),
            pl.BlockSpec((1, span, BLOCK), lambda s, n: (s, n, 0)),
        ],
        out_shape=[
            jax.ShapeDtypeStruct((batch * d, sub_len, ATT_WIDTH), jnp.bfloat16),
            jax.ShapeDtypeStruct((batch * d, sub_len, BLOCK), jnp.float32),
        ],
        scratch_shapes=[
            pltpu.VMEM((ATT_WIDTH, BLOCK + span), jnp.bfloat16),
            pltpu.VMEM((BLOCK + span, ATT_WIDTH), jnp.bfloat16),
        ],
        compiler_params=_params(2),
        name=f"dilated_attn_g{g}",
    )(qv3, qv3, kt, kt, qv3, qv3, qv3, qv3, bias)


def _cast_weight_once(wb_ref, w_ref, rows=256):
    @pl.when(pl.program_id(0) == 0)
    def _():
        for r0 in range(0, w_ref.shape[0], rows):
            wb_ref[r0:r0 + rows, :] = w_ref[r0:r0 + rows, :].astype(wb_ref.dtype)


def _a_out_kernel(o0_ref, o1_ref, o2_ref, l0_ref, l1_ref, l2_ref, z_ref, x_ref, w_ref,
                  out_ref, l1n_ref, l2n_ref, t1_ref, t2_ref, y_ref, wb_ref):
    bm = x_ref.shape[0]
    _cast_weight_once(wb_ref, w_ref)
    for r in range(4):
        l1n_ref[pl.ds(r, bm // 4, stride=4), :] = l1_ref[0, r]
    for r in range(16):
        l2n_ref[pl.ds(r, bm // 16, stride=16), :] = l2_ref[0, r]
    l0, l1, l2 = l0_ref[...], l1n_ref[...], l2n_ref[...]
    mx = jnp.maximum(jnp.maximum(l0, l1), l2)
    e0, e1, e2 = jnp.exp(l0 - mx), jnp.exp(l1 - mx), jnp.exp(l2 - mx)
    den = e0 + e1 + e2
    w0, w1, w2 = e0 / den, e1 / den, e2 / den
    acc = x_ref[...]
    for hh in range(N_HEADS):
        sl = slice(hh * HEAD_DIM, (hh + 1) * HEAD_DIM)
        for r in range(4):
            t1_ref[hh, pl.ds(r, bm // 4, stride=4), :] = o1_ref[0, r, :, sl].astype(jnp.float32)
        for r in range(16):
            t2_ref[hh, pl.ds(r, bm // 16, stride=16), :] = o2_ref[0, r, :, sl].astype(jnp.float32)
        o = (w0[:, hh:hh + 1] * o0_ref[:, sl].astype(jnp.float32)
             + w1[:, hh:hh + 1] * t1_ref[hh]
             + w2[:, hh:hh + 1] * t2_ref[hh])
        y_ref[:, sl] = (o * jax.nn.silu(z_ref[:, sl].astype(jnp.float32))).astype(y_ref.dtype)
        if hh % K_CHUNK_HEADS == K_CHUNK_HEADS - 1:
            ks = slice((hh + 1 - K_CHUNK_HEADS) * HEAD_DIM, (hh + 1) * HEAD_DIM)
            acc = acc + jnp.dot(y_ref[:, ks], wb_ref[ks, :], preferred_element_type=jnp.float32)
    out_ref[...] = acc


def _a_out(os, lses, p0, x2, w_out, batch, seq, bm=256):
    m = x2.shape[0]
    per_seq = seq // bm
    z_block = 2
    row = lambda i: (i, 0)

    def perm_spec(d, width):
        return pl.BlockSpec((1, d, bm // d, width), lambda i: (i // per_seq, 0, i % per_seq, 0))

    def perm_view(a, d):
        return a.reshape(batch, d, seq // d, a.shape[-1])

    return pl.pallas_call(
        _a_out_kernel,
        grid=(m // bm,),
        in_specs=[
            pl.BlockSpec((bm, ATT_WIDTH), row), perm_spec(4, ATT_WIDTH), perm_spec(16, ATT_WIDTH),
            pl.BlockSpec((bm, BLOCK), row), perm_spec(4, BLOCK), perm_spec(16, BLOCK),
            pl.BlockSpec((bm, ATT_WIDTH), lambda i: (i, z_block)),
            pl.BlockSpec((bm, D_MODEL), row),
            pl.BlockSpec((ATT_WIDTH, D_MODEL), lambda i: (0, 0), pipeline_mode=pl.Buffered(1)),
        ],
        out_specs=pl.BlockSpec((bm, D_MODEL), row),
        out_shape=jax.ShapeDtypeStruct((m, D_MODEL), jnp.float32),
        scratch_shapes=[
            pltpu.VMEM((bm, BLOCK), jnp.float32),
            pltpu.VMEM((bm, BLOCK), jnp.float32),
            pltpu.VMEM((N_HEADS, bm, HEAD_DIM), jnp.float32),
            pltpu.VMEM((N_HEADS, bm, HEAD_DIM), jnp.float32),
            pltpu.VMEM((bm, ATT_WIDTH), jnp.bfloat16),
            pltpu.VMEM((ATT_WIDTH, D_MODEL), jnp.bfloat16),
        ],
        compiler_params=_params(1),
        name="a_merge_out_proj",
    )(os[0].reshape(m, ATT_WIDTH), perm_view(os[1], 4), perm_view(os[2], 16),
      lses[0].reshape(m, BLOCK), perm_view(lses[1], 4), perm_view(lses[2], 16),
      p0, x2, w_out)


def _b_in_kernel(x_ref, g_ref, w_ref, b_ref, u_ref, sz_ref, h_ref, *, bn):
    bm, ch = u_ref.shape
    rows = 128
    for r0 in range(0, bm, rows):
        rs = slice(r0, r0 + rows)
        xc = x_ref[rs, :]
        ms = jnp.mean(xc * xc, axis=-1, keepdims=True)
        h_ref[rs, :] = (xc * lax.rsqrt(ms + EPS) * g_ref[...]).astype(h_ref.dtype)
    h = h_ref[...]
    for c0 in range(0, ch, bn):
        ca, cg, cz = (slice(s * ch + c0, s * ch + c0 + bn) for s in range(3))
        cs = slice(c0, c0 + bn)
        a = jnp.dot(h, w_ref[:, ca], preferred_element_type=jnp.float32) + b_ref[:, ca]
        ga = jnp.dot(h, w_ref[:, cg], preferred_element_type=jnp.float32) + b_ref[:, cg]
        u_ref[:, cs] = a * jax.nn.sigmoid(ga)
        z = jnp.dot(h, w_ref[:, cz], preferred_element_type=jnp.float32) + b_ref[:, cz]
        sz_ref[:, cs] = jax.nn.silu(z).astype(sz_ref.dtype)


def _b_in(x2, g, w, b, bm=256, bn=512):
    m, k = x2.shape
    ch = w.shape[1] // 3
    row = lambda i: (i, 0)
    const2 = lambda i: (0, 0)
    return pl.pallas_call(
        functools.partial(_b_in_kernel, bn=bn),
        grid=(m // bm,),
        in_specs=[
            pl.BlockSpec((bm, k), row),
            pl.BlockSpec((1, k), const2),
            pl.BlockSpec(w.shape, const2, pipeline_mode=pl.Buffered(1)),
            pl.BlockSpec(b.shape, const2),
        ],
        out_specs=[pl.BlockSpec((bm, ch), row), pl.BlockSpec((bm, ch), row)],
        out_shape=[
            jax.ShapeDtypeStruct((m, ch), jnp.float32),
            jax.ShapeDtypeStruct((m, ch), jnp.bfloat16),
        ],
        scratch_shapes=[pltpu.VMEM((bm, k), jnp.bfloat16)],
        compiler_params=_params(1),
        name="b_in_proj_glu",
    )(x2, g, w, b)


def _b_out_kernel(u_ref, halo_ref, sz_ref, x_ref, cw_ref, cb_ref, lg_ref, lb_ref,
                  w_ref, bo_ref, out_ref, ext_ref, conv_ref, y_ref, wb_ref, *, blocks_per_seq):
    bm = u_ref.shape[0]
    _cast_weight_once(wb_ref, w_ref)
    n_slabs = u_ref.shape[1] // 128
    i = pl.program_id(0)
    seq_start = (i % blocks_per_seq) == 0

    for c in range(n_slabs):
        sl = slice(c * 128, (c + 1) * 128)
        ext_ref[c, CONV_HALO:CONV_HALO + bm, :] = u_ref[:, sl]

    @pl.when(seq_start)
    def _():
        ext_ref[:, 0:CONV_HALO, :] = jnp.zeros((n_slabs, CONV_HALO, 128), jnp.float32)

    @pl.when(jnp.logical_not(seq_start))
    def _():
        for c in range(n_slabs):
            ext_ref[c, 0:CONV_HALO, :] = halo_ref[:, c * 128:(c + 1) * 128]

    first_tap = CONV_HALO - (CONV_WIDTH - 1)
    chunks = bm // CONV_ROWS

    def conv_body(t, carry):
        c = t // chunks
        r0 = pl.multiple_of((t % chunks) * CONV_ROWS, CONV_ROWS)
        acc = jnp.zeros((CONV_ROWS, 128), jnp.float32)
        for kk in range(CONV_WIDTH):
            acc = acc + cw_ref[c, kk:kk + 1, :] * ext_ref[c, pl.ds(r0 + first_tap + kk, CONV_ROWS), :]
        conv_ref[c, pl.ds(r0, CONV_ROWS), :] = acc
        return carry

    lax.fori_loop(0, n_slabs * chunks, conv_body, 0, unroll=4)

    tot = jnp.zeros((bm, 128), jnp.float32)
    for c in range(n_slabs):
        sl = slice(c * 128, (c + 1) * 128)
        tot = tot + (conv_ref[c] + cb_ref[:, sl])
    mu = jnp.sum(tot, axis=-1, keepdims=True) / (n_slabs * 128)
    sq = jnp.zeros((bm, 128), jnp.float32)
    for c in range(n_slabs):
        sl = slice(c * 128, (c + 1) * 128)
        dv = conv_ref[c] + cb_ref[:, sl] - mu
        sq = sq + dv * dv
    var = jnp.sum(sq, axis=-1, keepdims=True) / (n_slabs * 128)
    inv = lax.rsqrt(var + EPS)
    acc = x_ref[...] + bo_ref[...]
    for c in range(n_slabs):
        sl = slice(c * 128, (c + 1) * 128)
        uf = (conv_ref[c] + cb_ref[:, sl] - mu) * inv * lg_ref[:, sl] + lb_ref[:, sl]
        y_ref[:, sl] = (jax.nn.silu(uf) * sz_ref[:, sl].astype(jnp.float32)).astype(y_ref.dtype)
        if c % K_CHUNK_HEADS == K_CHUNK_HEADS - 1:
            ks = slice((c + 1 - K_CHUNK_HEADS) * 128, (c + 1) * 128)
            acc = acc + jnp.dot(y_ref[:, ks], wb_ref[ks, :], preferred_element_type=jnp.float32)
    out_ref[...] = acc


def _b_out(u, sz, x2, conv_w_slabs, conv_b, ln_g, ln_b, w_out, b_out, seq, bm=256):
    m, ch = u.shape
    n_slabs = ch // 128
    halo_blocks = bm // CONV_HALO
    row = lambda i: (i, 0)
    const2 = lambda i: (0, 0)
    return pl.pallas_call(
        functools.partial(_b_out_kernel, blocks_per_seq=seq // bm),
        grid=(m // bm,),
        in_specs=[
            pl.BlockSpec((bm, ch), row),
            pl.BlockSpec((CONV_HALO, ch), lambda i: (jnp.maximum(i * halo_blocks - 1, 0), 0)),
            pl.BlockSpec((bm, ch), row),
            pl.BlockSpec((bm, D_MODEL), row),
            pl.BlockSpec((n_slabs, CONV_HALO, 128), lambda i: (0, 0, 0)),
            pl.BlockSpec((1, ch), const2), pl.BlockSpec((1, ch), const2),
            pl.BlockSpec((1, ch), const2),
            pl.BlockSpec((ch, D_MODEL), const2, pipeline_mode=pl.Buffered(1)),
            pl.BlockSpec((1, D_MODEL), const2),
        ],
        out_specs=pl.BlockSpec((bm, D_MODEL), row),
        out_shape=jax.ShapeDtypeStruct((m, D_MODEL), jnp.float32),
        scratch_shapes=[
            pltpu.VMEM((n_slabs, CONV_HALO + bm, 128), jnp.float32),
            pltpu.VMEM((n_slabs, bm, 128), jnp.float32),
            pltpu.VMEM((bm, ch), jnp.bfloat16),
            pltpu.VMEM((ch, D_MODEL), jnp.bfloat16),
        ],
        compiler_params=_params(1),
        name="b_conv_ln_out_proj",
    )(u, u, sz, x2, conv_w_slabs, conv_b, ln_g, ln_b, w_out, b_out)


def kernel(x, norm_g, rel_bias, a_w_in, a_q_gain, a_k_gain, a_w_out, b_w_in, b_b_in,
           b_conv_w, b_conv_b, b_ln_g, b_ln_b, b_w_out, b_b_out):
    batch, seq, dm = x.shape
    m = batch * seq
    bf16 = jnp.bfloat16
    x2 = x.reshape(m, dm)

    ones = jnp.ones((ATT_WIDTH,), jnp.float32)
    cols = []
    for g in range(N_GROUPS):
        cols += [jnp.tile(a_q_gain[0, g], N_HEADS) * (HEAD_DIM ** -0.5 * LOG2E), ones, ones]
    cols.append(ones)
    colscale = jnp.concatenate(cols).reshape(1, A_COLS)
    k_gain_tile = jnp.broadcast_to(a_k_gain[0][:, :, None], (N_GROUPS, HEAD_DIM, 128))

    w_in = a_w_in[0]
    hs = _a_norm(x2, norm_g[0:1], batch, seq)
    bias = _expand_bias(rel_bias, _bucket_tiles())
    os, lses, p0 = [], [], None
    for g in range(N_GROUPS):
        hg = hs[g].reshape(m, dm)
        qv = _a_in(hg, w_in, colscale, g, with_gate=(g == 0))
        kt = _a_in_k(hg, w_in, k_gain_tile, g)
        if g == 0:
            p0 = qv
        o, lse = _attention(qv, kt, bias, g, batch, seq)
        os.append(o)
        lses.append(lse)
    x2 = _a_out(os, lses, p0, x2, a_w_out[0], batch, seq)

    u, sz = _b_in(x2, norm_g[1:2], b_w_in[0].astype(bf16), b_b_in[0:1])
    ch = u.shape[1]
    cw = jnp.pad(b_conv_w[0], ((0, CONV_HALO - CONV_WIDTH), (0, 0)))
    cw = cw.reshape(CONV_HALO, ch // 128, 128).transpose(1, 0, 2)
    x2 = _b_out(u, sz, x2, cw, b_conv_b[0:1], b_ln_g[0:1], b_ln_b[0:1],
                b_w_out[0], b_b_out[0:1], seq)
    return x2.reshape(batch, seq, dm)
```

```python
import functools
import math

import jax
import jax.numpy as jnp
from jax import lax
from jax.experimental import pallas as pl
from jax.experimental.pallas import tpu as pltpu

D_MODEL = 2048
HEAD_DIM = 128
N_HEADS = 16
ATT_GROUPS = ((128, 1), (512, 4), (2048, 16))
N_GROUPS = 3
ATT_WIDTH = N_HEADS * HEAD_DIM
A_COLS = 3 * N_GROUPS * ATT_WIDTH + ATT_WIDTH
BLOCK = 128
N_BUCKETS = 32
MAX_DISTANCE = 2048
CONV_WIDTH = 31
EPS = 1e-6
NEG = -1e30
LOG2E = math.log2(math.e)
LN2 = math.log(2.0)

VMEM_LIMIT_BYTES = 56 * 1024 * 1024
CONV_HALO = 32
CONV_ROWS = 64
ATT_QB = 8
K_CHUNK_HEADS = 2


def _params(n_axes):
    return pltpu.CompilerParams(
        dimension_semantics=("arbitrary",) * n_axes,
        vmem_limit_bytes=VMEM_LIMIT_BYTES)


def _rmsnorm_to(h_ref, x_ref, g_ref, rows):
    n_chunks = x_ref.shape[0] // rows

    def body(c, carry):
        r0 = pl.multiple_of(c * rows, rows)
        xc = x_ref[pl.ds(r0, rows), :]
        ms = jnp.mean(xc * xc, axis=-1, keepdims=True)
        h_ref[pl.ds(r0, rows), :] = (xc * lax.rsqrt(ms + EPS) * g_ref[...]).astype(h_ref.dtype)
        return carry

    lax.fori_loop(0, n_chunks, body, 0)


def _a_norm_kernel(x_ref, g_ref, hn_ref, h4_ref, h16_ref, slab_ref):
    bm = x_ref.shape[0]
    n_slabs = x_ref.shape[1] // 128
    rows = 128
    for c0 in range(bm // rows):
        rs = slice(c0 * rows, (c0 + 1) * rows)
        xc = x_ref[rs, :]
        ms = jnp.mean(xc * xc, axis=-1, keepdims=True)
        hn = xc * lax.rsqrt(ms + EPS) * g_ref[...]
        hn_ref[rs, :] = hn.astype(hn_ref.dtype)
        for c in range(n_slabs):
            slab_ref[c, rs, :] = hn[:, c * 128:(c + 1) * 128]
    for c in range(n_slabs):
        sl = slice(c * 128, (c + 1) * 128)
        for r in range(4):
            h4_ref[0, r, :, sl] = slab_ref[c, pl.ds(r, bm // 4, stride=4), :].astype(h4_ref.dtype)
        for r in range(16):
            h16_ref[0, r, :, sl] = slab_ref[c, pl.ds(r, bm // 16, stride=16), :].astype(h16_ref.dtype)


def _a_norm(x2, g, batch, seq, bm=1024):
    m, k = x2.shape
    per_seq = seq // bm
    bf16 = jnp.bfloat16

    def perm_spec(d):
        return pl.BlockSpec((1, d, bm // d, k), lambda i: (i // per_seq, 0, i % per_seq, 0))

    return pl.pallas_call(
        _a_norm_kernel,
        grid=(m // bm,),
        in_specs=[pl.BlockSpec((bm, k), lambda i: (i, 0)), pl.BlockSpec((1, k), lambda i: (0, 0))],
        out_specs=[pl.BlockSpec((bm, k), lambda i: (i, 0)), perm_spec(4), perm_spec(16)],
        out_shape=[
            jax.ShapeDtypeStruct((m, k), bf16),
            jax.ShapeDtypeStruct((batch, 4, seq // 4, k), bf16),
            jax.ShapeDtypeStruct((batch, 16, seq // 16, k), bf16),
        ],
        scratch_shapes=[pltpu.VMEM((k // 128, bm, 128), jnp.float32)],
        compiler_params=_params(1),
        name="a_rmsnorm_permute",
    )(x2, g)


def _a_in_kernel(h_ref, wl_ref, wr_ref, cs_ref, o_ref, wb_ref, *, n_qv_steps):
    half = wl_ref.shape[1]

    @pl.when(pl.program_id(1) == 0)
    def _():
        rows = 256
        for r0 in range(0, wl_ref.shape[0], rows):
            wb_ref[r0:r0 + rows, 0:half] = wl_ref[r0:r0 + rows, :].astype(wb_ref.dtype)
            wb_ref[r0:r0 + rows, half:2 * half] = wr_ref[r0:r0 + rows, :].astype(wb_ref.dtype)

    acc = jnp.dot(h_ref[...], wb_ref[...], preferred_element_type=jnp.float32)
    is_q = pl.program_id(0) < n_qv_steps
    for hh in range(half // HEAD_DIM):
        sl = slice(hh * HEAD_DIM, (hh + 1) * HEAD_DIM)
        a = acc[:, sl]
        ms = jnp.mean(a * a, axis=-1, keepdims=True)
        scale = jnp.where(is_q, lax.rsqrt(ms + EPS) * cs_ref[:, sl], 1.0)
        o_ref[:, sl] = (a * scale).astype(o_ref.dtype)
    o_ref[:, half:2 * half] = acc[:, half:2 * half].astype(o_ref.dtype)


def _a_in(h, w, colscale, g, with_gate, bm=1024, half=1024):
    m, k = h.shape
    per_section = ATT_WIDTH // half
    n_steps = per_section + (1 if with_gate else 0) * (per_section // 2)
    q0 = 3 * g * per_section
    v0 = q0 + 2 * per_section
    gate0 = 3 * N_GROUPS * per_section

    def left(j, i):
        return (0, jnp.where(j < per_section, q0 + j, gate0 + 2 * (j - per_section)))

    def right(j, i):
        return (0, jnp.where(j < per_section, v0 + j, gate0 + 2 * (j - per_section) + 1))

    return pl.pallas_call(
        functools.partial(_a_in_kernel, n_qv_steps=per_section),
        grid=(n_steps, m // bm),
        in_specs=[
            pl.BlockSpec((bm, k), lambda j, i: (i, 0)),
            pl.BlockSpec((k, half), left, pipeline_mode=pl.Buffered(1)),
            pl.BlockSpec((k, half), right, pipeline_mode=pl.Buffered(1)),
            pl.BlockSpec((1, half), left),
        ],
        out_specs=pl.BlockSpec((bm, 2 * half), lambda j, i: (i, j)),
        out_shape=jax.ShapeDtypeStruct((m, n_steps * 2 * half), jnp.bfloat16),
        scratch_shapes=[pltpu.VMEM((k, 2 * half), jnp.bfloat16)],
        compiler_params=_params(2),
        name=f"a_in_proj_qv_g{g}",
    )(h, w, w, colscale)


def _a_in_k_kernel(h_ref, w_ref, gain_ref, o_ref, wt_ref):
    @pl.when(pl.program_id(1) == 0)
    def _():
        rows = 256
        for c0 in range(w_ref.shape[0] // rows):
            rs = slice(c0 * rows, (c0 + 1) * rows)
            wt_ref[:, rs] = w_ref[rs, :].T.astype(wt_ref.dtype)

    acc = lax.dot_general(wt_ref[...], h_ref[...], (((1,), (1,)), ((), ())),
                          preferred_element_type=jnp.float32)
    bm = acc.shape[1]
    for hh in range(acc.shape[0] // HEAD_DIM):
        hs = slice(hh * HEAD_DIM, (hh + 1) * HEAD_DIM)
        a = acc[hs, :]
        ms = jnp.mean(a * a, axis=0, keepdims=True)
        an = a * lax.rsqrt(ms + EPS)
        for lb in range(bm // 128):
            ls = slice(lb * 128, (lb + 1) * 128)
            o_ref[hs, ls] = (an[:, ls] * gain_ref[0]).astype(o_ref.dtype)


def _a_in_k(h, w, gain_tile, g, bm=1024, bn=1024):
    m, k = h.shape
    per_section = ATT_WIDTH // bn
    k0 = (3 * g + 1) * per_section
    return pl.pallas_call(
        _a_in_k_kernel,
        grid=(per_section, m // bm),
        in_specs=[
            pl.BlockSpec((bm, k), lambda j, i: (i, 0)),
            pl.BlockSpec((k, bn), lambda j, i: (0, k0 + j)),
            pl.BlockSpec((1, HEAD_DIM, 128), lambda j, i: (g, 0, 0)),
        ],
        out_specs=pl.BlockSpec((bn, bm), lambda j, i: (j, i)),
        out_shape=jax.ShapeDtypeStruct((ATT_WIDTH, m), jnp.bfloat16),
        scratch_shapes=[pltpu.VMEM((bn, k), jnp.bfloat16)],
        compiler_params=_params(2),
        name=f"a_in_proj_kT_g{g}",
    )(h, w, gain_tile)


def _bias_kernel(table_ref, bucket_ref, o_ref):
    g = pl.program_id(0)
    hh = pl.program_id(1)
    bucket = bucket_ref[0]
    acc = jnp.full(bucket.shape, NEG, jnp.float32)
    for b in range(N_BUCKETS):
        acc = jnp.where(bucket == b, table_ref[b, g * N_HEADS + hh] * LOG2E, acc)
    col = lax.broadcasted_iota(jnp.int32, bucket.shape, 1)
    o_ref[0, 0, 0] = acc
    o_ref[0, 1, 0] = jnp.where(col < BLOCK, NEG, acc)


def _expand_bias(rel_bias, bucket):
    return pl.pallas_call(
        _bias_kernel,
        grid=(N_GROUPS, N_HEADS),
        in_specs=[
            pl.BlockSpec(memory_space=pltpu.SMEM),
            pl.BlockSpec((1, BLOCK, 2 * BLOCK), lambda g, h: (g, 0, 0)),
        ],
        out_specs=pl.BlockSpec((1, 2, 1, BLOCK, 2 * BLOCK), lambda g, h: (g, 0, h, 0, 0)),
        out_shape=jax.ShapeDtypeStruct((N_GROUPS, 2, N_HEADS, BLOCK, 2 * BLOCK), jnp.float32),
        compiler_params=_params(2),
        name="rel_bias_expand",
    )(rel_bias, bucket)


def _bucket_tiles():
    max_exact = N_BUCKETS // 2
    qi = jnp.arange(BLOCK)[:, None] + BLOCK
    kj = jnp.arange(2 * BLOCK)[None, :]
    step = qi - kj
    tiles = []
    for window, dilation in ATT_GROUPS:
        steps = window // dilation
        in_window = (step >= 0) & (step <= steps)
        dist = jnp.maximum(step, 0) * dilation
        is_small = dist < max_exact
        ratio = jnp.log(jnp.maximum(dist, 1).astype(jnp.float32) / max_exact) / math.log(MAX_DISTANCE / max_exact)
        large = max_exact + (ratio * (N_BUCKETS - max_exact)).astype(jnp.int32)
        large = jnp.minimum(large, N_BUCKETS - 1)
        bucket = jnp.where(is_small, dist, large)
        tiles.append(jnp.where(in_window, bucket, -1).astype(jnp.int32))
    return jnp.stack(tiles, axis=0)


def _attn_kernel(qa_ref, qb_ref, kt_ref, ktp_ref, va_ref, vb_ref, vpa_ref, vpb_ref, bias_ref,
                 o_ref, lse_ref, kk_ref, vv_ref):
    n = pl.program_id(1)
    span = qa_ref.shape[1]
    half = qa_ref.shape[2]
    q_refs = (qa_ref, qb_ref)
    kk_ref[:, 0:BLOCK] = ktp_ref[...]
    kk_ref[:, BLOCK:BLOCK + span] = kt_ref[...]
    vv_ref[0:BLOCK, 0:half] = vpa_ref[0]
    vv_ref[0:BLOCK, half:2 * half] = vpb_ref[0]
    vv_ref[BLOCK:BLOCK + span, 0:half] = va_ref[0]
    vv_ref[BLOCK:BLOCK + span, half:2 * half] = vb_ref[0]

    lane = lax.broadcasted_iota(jnp.int32, (BLOCK, BLOCK), 1)
    for sb in range(span // BLOCK):
        rows = slice(sb * BLOCK, (sb + 1) * BLOCK)
        keys = slice(sb * BLOCK, (sb + 2) * BLOCK)
        variant = jnp.where(n == 0, 1, 0) if sb == 0 else 0
        lse_tile = jnp.zeros((BLOCK, BLOCK), jnp.float32)
        for hh in range(N_HEADS):
            sl = slice(hh * HEAD_DIM, (hh + 1) * HEAD_DIM)
            q_ref = q_refs[(hh * HEAD_DIM) // half]
            qs = slice((hh * HEAD_DIM) % half, (hh * HEAD_DIM) % half + HEAD_DIM)
            s = jnp.dot(q_ref[0, rows, qs], kk_ref[sl, keys], preferred_element_type=jnp.float32)
            s = s + bias_ref[0, variant, hh]
            m = jnp.max(s, axis=-1, keepdims=True)
            p = jnp.exp2(s - m)
            l = jnp.sum(p, axis=-1, keepdims=True)
            o = jnp.dot(p.astype(vv_ref.dtype), vv_ref[keys, sl], preferred_element_type=jnp.float32)
            o_ref[0, rows, sl] = (o / l).astype(o_ref.dtype)
            lse_tile = jnp.where(lane == hh, (m + jnp.log2(l)) * LN2, lse_tile)
        lse_ref[0, rows, :] = lse_tile


def _attention(qv, kt, bias, g, batch, seq):
    _, d = ATT_GROUPS[g]
    sub_len = seq // d
    qb = min(ATT_QB, sub_len // BLOCK)
    span = qb * BLOCK
    nb = sub_len // span
    qv3 = qv.reshape(batch * d, sub_len, qv.shape[1])
    half = ATT_WIDTH // 2

    def cur_spec(col):
        return pl.BlockSpec((1, span, half), lambda s, n: (s, n, col))

    def prev_spec(col):
        return pl.BlockSpec((1, BLOCK, half), lambda s, n: (s, jnp.maximum(n * qb - 1, 0), col))

    return pl.pallas_call(
        _attn_kernel,
        grid=(batch * d, nb),
        in_specs=[
            cur_spec(0), cur_spec(2),
            pl.BlockSpec((ATT_WIDTH, span), lambda s, n: (0, s * nb + n)),
            pl.BlockSpec((ATT_WIDTH, BLOCK),
                         lambda s, n: (0, jnp.maximum((s * nb + n) * qb - 1, 0))),
            cur_spec(1), cur_spec(3), prev_spec(1), prev_spec(3),
            pl.BlockSpec((1, 2, N_HEADS, BLOCK, 2 * BLOCK), lambda s, n: (g, 0, 0, 0, 0)),
        ],
        out_specs=[
            pl.BlockSpec((1, span, ATT_WIDTH), lambda s, n: (s, n, 0)),
            pl.BlockSpec((1, span, BLOCK), lambda s, n: (s, n, 0)),
        ],
        out_shape=[
            jax.ShapeDtypeStruct((batch * d, sub_len, ATT_WIDTH), jnp.bfloat16),
            jax.ShapeDtypeStruct((batch * d, sub_len, BLOCK), jnp.float32),
        ],
        scratch_shapes=[
            pltpu.VMEM((ATT_WIDTH, BLOCK + span), jnp.bfloat16),
            pltpu.VMEM((BLOCK + span, ATT_WIDTH), jnp.bfloat16),
        ],
        compiler_params=_params(2),
        name=f"dilated_attn_g{g}",
    )(qv3, qv3, kt, kt, qv3, qv3, qv3, qv3, bias)


def _cast_weight_once(wb_ref, w_ref, rows=256):
    @pl.when(pl.program_id(0) == 0)
    def _():
        for r0 in range(0, w_ref.shape[0], rows):
            wb_ref[r0:r0 + rows, :] = w_ref[r0:r0 + rows, :].astype(wb_ref.dtype)


def _a_out_kernel(o0_ref, o1_ref, o2_ref, l0_ref, l1_ref, l2_ref, z_ref, x_ref, w_ref,
                  out_ref, l1n_ref, l2n_ref, t1_ref, t2_ref, y_ref, wb_ref):
    bm = x_ref.shape[0]
    _cast_weight_once(wb_ref, w_ref)
    for r in range(4):
        l1n_ref[pl.ds(r, bm // 4, stride=4), :] = l1_ref[0, r]
    for r in range(16):
        l2n_ref[pl.ds(r, bm // 16, stride=16), :] = l2_ref[0, r]
    l0, l1, l2 = l0_ref[...], l1n_ref[...], l2n_ref[...]
    mx = jnp.maximum(jnp.maximum(l0, l1), l2)
    e0, e1, e2 = jnp.exp(l0 - mx), jnp.exp(l1 - mx), jnp.exp(l2 - mx)
    den = e0 + e1 + e2
    w0, w1, w2 = e0 / den, e1 / den, e2 / den
    acc = x_ref[...]
    for hh in range(N_HEADS):
        sl = slice(hh * HEAD_DIM, (hh + 1) * HEAD_DIM)
        for r in range(4):
            t1_ref[hh, pl.ds(r, bm // 4, stride=4), :] = o1_ref[0, r, :, sl].astype(jnp.float32)
        for r in range(16):
            t2_ref[hh, pl.ds(r, bm // 16, stride=16), :] = o2_ref[0, r, :, sl].astype(jnp.float32)
        o = (w0[:, hh:hh + 1] * o0_ref[:, sl].astype(jnp.float32)
             + w1[:, hh:hh + 1] * t1_ref[hh]
             + w2[:, hh:hh + 1] * t2_ref[hh])
        y_ref[:, sl] = (o * jax.nn.silu(z_ref[:, sl].astype(jnp.float32))).astype(y_ref.dtype)
        if hh % K_CHUNK_HEADS == K_CHUNK_HEADS - 1:
            ks = slice((hh + 1 - K_CHUNK_HEADS) * HEAD_DIM, (hh + 1) * HEAD_DIM)
            acc = acc + jnp.dot(y_ref[:, ks], wb_ref[ks, :], preferred_element_type=jnp.float32)
    out_ref[...] = acc


def _a_out(os, lses, p0, x2, w_out, batch, seq, bm=256):
    m = x2.shape[0]
    per_seq = seq // bm
    z_block = 2
    row = lambda i: (i, 0)

    def perm_spec(d, width):
        return pl.BlockSpec((1, d, bm // d, width), lambda i: (i // per_seq, 0, i % per_seq, 0))

    def perm_view(a, d):
        return a.reshape(batch, d, seq // d, a.shape[-1])

    return pl.pallas_call(
        _a_out_kernel,
        grid=(m // bm,),
        in_specs=[
            pl.BlockSpec((bm, ATT_WIDTH), row), perm_spec(4, ATT_WIDTH), perm_spec(16, ATT_WIDTH),
            pl.BlockSpec((bm, BLOCK), row), perm_spec(4, BLOCK), perm_spec(16, BLOCK),
            pl.BlockSpec((bm, ATT_WIDTH), lambda i: (i, z_block)),
            pl.BlockSpec((bm, D_MODEL), row),
            pl.BlockSpec((ATT_WIDTH, D_MODEL), lambda i: (0, 0), pipeline_mode=pl.Buffered(1)),
        ],
        out_specs=pl.BlockSpec((bm, D_MODEL), row),
        out_shape=jax.ShapeDtypeStruct((m, D_MODEL), jnp.float32),
        scratch_shapes=[
            pltpu.VMEM((bm, BLOCK), jnp.float32),
            pltpu.VMEM((bm, BLOCK), jnp.float32),
            pltpu.VMEM((N_HEADS, bm, HEAD_DIM), jnp.float32),
            pltpu.VMEM((N_HEADS, bm, HEAD_DIM), jnp.float32),
            pltpu.VMEM((bm, ATT_WIDTH), jnp.bfloat16),
            pltpu.VMEM((ATT_WIDTH, D_MODEL), jnp.bfloat16),
        ],
        compiler_params=_params(1),
        name="a_merge_out_proj",
    )(os[0].reshape(m, ATT_WIDTH), perm_view(os[1], 4), perm_view(os[2], 16),
      lses[0].reshape(m, BLOCK), perm_view(lses[1], 4), perm_view(lses[2], 16),
      p0, x2, w_out)


def _b_in_kernel(x_ref, g_ref, w_ref, b_ref, u_ref, sz_ref, h_ref, *, bn):
    bm, ch = u_ref.shape
    rows = 128
    for r0 in range(0, bm, rows):
        rs = slice(r0, r0 + rows)
        xc = x_ref[rs, :]
        ms = jnp.mean(xc * xc, axis=-1, keepdims=True)
        h_ref[rs, :] = (xc * lax.rsqrt(ms + EPS) * g_ref[...]).astype(h_ref.dtype)
    h = h_ref[...]
    for c0 in range(0, ch, bn):
        ca, cg, cz = (slice(s * ch + c0, s * ch + c0 + bn) for s in range(3))
        cs = slice(c0, c0 + bn)
        a = jnp.dot(h, w_ref[:, ca], preferred_element_type=jnp.float32) + b_ref[:, ca]
        ga = jnp.dot(h, w_ref[:, cg], preferred_element_type=jnp.float32) + b_ref[:, cg]
        u_ref[:, cs] = a * jax.nn.sigmoid(ga)
        z = jnp.dot(h, w_ref[:, cz], preferred_element_type=jnp.float32) + b_ref[:, cz]
        sz_ref[:, cs] = jax.nn.silu(z).astype(sz_ref.dtype)


def _b_in(x2, g, w, b, bm=256, bn=512):
    m, k = x2.shape
    ch = w.shape[1] // 3
    row = lambda i: (i, 0)
    const2 = lambda i: (0, 0)
    return pl.pallas_call(
        functools.partial(_b_in_kernel, bn=bn),
        grid=(m // bm,),
        in_specs=[
            pl.BlockSpec((bm, k), row),
            pl.BlockSpec((1, k), const2),
            pl.BlockSpec(w.shape, const2, pipeline_mode=pl.Buffered(1)),
            pl.BlockSpec(b.shape, const2),
        ],
        out_specs=[pl.BlockSpec((bm, ch), row), pl.BlockSpec((bm, ch), row)],
        out_shape=[
            jax.ShapeDtypeStruct((m, ch), jnp.float32),
            jax.ShapeDtypeStruct((m, ch), jnp.bfloat16),
        ],
        scratch_shapes=[pltpu.VMEM((bm, k), jnp.bfloat16)],
        compiler_params=_params(1),
        name="b_in_proj_glu",
    )(x2, g, w, b)


def _b_out_kernel(u_ref, halo_ref, sz_ref, x_ref, cw_ref, cb_ref, lg_ref, lb_ref,
                  w_ref, bo_ref, out_ref, ext_ref, conv_ref, y_ref, wb_ref, *, blocks_per_seq):
    bm = u_ref.shape[0]
    _cast_weight_once(wb_ref, w_ref)
    n_slabs = u_ref.shape[1] // 128
    i = pl.program_id(0)
    seq_start = (i % blocks_per_seq) == 0

    for c in range(n_slabs):
        sl = slice(c * 128, (c + 1) * 128)
        ext_ref[c, CONV_HALO:CONV_HALO + bm, :] = u_ref[:, sl]

    @pl.when(seq_start)
    def _():
        ext_ref[:, 0:CONV_HALO, :] = jnp.zeros((n_slabs, CONV_HALO, 128), jnp.float32)

    @pl.when(jnp.logical_not(seq_start))
    def _():
        for c in range(n_slabs):
            ext_ref[c, 0:CONV_HALO, :] = halo_ref[:, c * 128:(c + 1) * 128]

    first_tap = CONV_HALO - (CONV_WIDTH - 1)
    chunks = bm // CONV_ROWS

    def conv_body(t, carry):
        c = t // chunks
        r0 = pl.multiple_of((t % chunks) * CONV_ROWS, CONV_ROWS)
        acc = jnp.zeros((CONV_ROWS, 128), jnp.float32)
        for kk in range(CONV_WIDTH):
            acc = acc + cw_ref[c, kk:kk + 1, :] * ext_ref[c, pl.ds(r0 + first_tap + kk, CONV_ROWS), :]
        conv_ref[c, pl.ds(r0, CONV_ROWS), :] = acc
        return carry

    lax.fori_loop(0, n_slabs * chunks, conv_body, 0, unroll=4)

    tot = jnp.zeros((bm, 128), jnp.float32)
    for c in range(n_slabs):
        sl = slice(c * 128, (c + 1) * 128)
        tot = tot + (conv_ref[c] + cb_ref[:, sl])
    mu = jnp.sum(tot, axis=-1, keepdims=True) / (n_slabs * 128)
    sq = jnp.zeros((bm, 128), jnp.float32)
    for c in range(n_slabs):
        sl = slice(c * 128, (c + 1) * 128)
        dv = conv_ref[c] + cb_ref[:, sl] - mu
        sq = sq + dv * dv
    var = jnp.sum(sq, axis=-1, keepdims=True) / (n_slabs * 128)
    inv = lax.rsqrt(var + EPS)
    acc = x_ref[...] + bo_ref[...]
    for c in range(n_slabs):
        sl = slice(c * 128, (c + 1) * 128)
        uf = (conv_ref[c] + cb_ref[:, sl] - mu) * inv * lg_ref[:, sl] + lb_ref[:, sl]
        y_ref[:, sl] = (jax.nn.silu(uf) * sz_ref[:, sl].astype(jnp.float32)).astype(y_ref.dtype)
        if c % K_CHUNK_HEADS == K_CHUNK_HEADS - 1:
            ks = slice((c + 1 - K_CHUNK_HEADS) * 128, (c + 1) * 128)
            acc = acc + jnp.dot(y_ref[:, ks], wb_ref[ks, :], preferred_element_type=jnp.float32)
    out_ref[...] = acc


def _b_out(u, sz, x2, conv_w_slabs, conv_b, ln_g, ln_b, w_out, b_out, seq, bm=256):
    m, ch = u.shape
    n_slabs = ch // 128
    halo_blocks = bm // CONV_HALO
    row = lambda i: (i, 0)
    const2 = lambda i: (0, 0)
    return pl.pallas_call(
        functools.partial(_b_out_kernel, blocks_per_seq=seq // bm),
        grid=(m // bm,),
        in_specs=[
            pl.BlockSpec((bm, ch), row),
            pl.BlockSpec((CONV_HALO, ch), lambda i: (jnp.maximum(i * halo_blocks - 1, 0), 0)),
            pl.BlockSpec((bm, ch), row),
            pl.BlockSpec((bm, D_MODEL), row),
            pl.BlockSpec((n_slabs, CONV_HALO, 128), lambda i: (0, 0, 0)),
            pl.BlockSpec((1, ch), const2), pl.BlockSpec((1, ch), const2),
            pl.BlockSpec((1, ch), const2),
            pl.BlockSpec((ch, D_MODEL), const2, pipeline_mode=pl.Buffered(1)),
            pl.BlockSpec((1, D_MODEL), const2),
        ],
        out_specs=pl.BlockSpec((bm, D_MODEL), row),
        out_shape=jax.ShapeDtypeStruct((m, D_MODEL), jnp.float32),
        scratch_shapes=[
            pltpu.VMEM((n_slabs, CONV_HALO + bm, 128), jnp.float32),
            pltpu.VMEM((n_slabs, bm, 128), jnp.float32),
            pltpu.VMEM((bm, ch), jnp.bfloat16),
            pltpu.VMEM((ch, D_MODEL), jnp.bfloat16),
        ],
        compiler_params=_params(1),
        name="b_conv_ln_out_proj",
    )(u, u, sz, x2, conv_w_slabs, conv_b, ln_g, ln_b, w_out, b_out)


def kernel(x, norm_g, rel_bias, a_w_in, a_q_gain, a_k_gain, a_w_out, b_w_in, b_b_in,
           b_conv_w, b_conv_b, b_ln_g, b_ln_b, b_w_out, b_b_out):
    batch, seq, dm = x.shape
    m = batch * seq
    bf16 = jnp.bfloat16
    x2 = x.reshape(m, dm)

    ones = jnp.ones((ATT_WIDTH,), jnp.float32)
    cols = []
    for g in range(N_GROUPS):
        cols += [jnp.tile(a_q_gain[0, g], N_HEADS) * (HEAD_DIM ** -0.5 * LOG2E), ones, ones]
    cols.append(ones)
    colscale = jnp.concatenate(cols).reshape(1, A_COLS)
    k_gain_tile = jnp.broadcast_to(a_k_gain[0][:, :, None], (N_GROUPS, HEAD_DIM, 128))

    w_in = a_w_in[0]
    hs = _a_norm(x2, norm_g[0:1], batch, seq)
    bias = _expand_bias(rel_bias, _bucket_tiles())
    os, lses, p0 = [], [], None
    for g in range(N_GROUPS):
        hg = hs[g].reshape(m, dm)
        qv = _a_in(hg, w_in, colscale, g, with_gate=(g == 0))
        kt = _a_in_k(hg, w_in, k_gain_tile, g)
        if g == 0:
            p0 = qv
        o, lse = _attention(qv, kt, bias, g, batch, seq)
        os.append(o)
        lses.append(lse)
    x2 = _a_out(os, lses, p0, x2, a_w_out[0], batch, seq)

    u, sz = _b_in(x2, norm_g[1:2], b_w_in[0].astype(bf16), b_b_in[0:1])
    ch = u.shape[1]
    cw = jnp.pad(b_conv_w[0], ((0, CONV_HALO - CONV_WIDTH), (0, 0)))
    cw = cw.reshape(CONV_HALO, ch // 128, 128).transpose(1, 0, 2)
    x2 = _b_out(u, sz, x2, cw, b_conv_b[0:1], b_ln_g[0:1], b_ln_b[0:1],
                b_w_out[0], b_b_out[0:1], seq)
    return x2.reshape(batch, seq, dm)
```
